```python
import jax, jax.numpy as jnp
from jax import lax
import numpy as np

D_MODEL = 1024
BATCH = 4
SEQ = 8192
DEPTH = 4

GRID_W = 64
NA_HEADS = 8
NA_HEAD_DIM = 64
NA_WIDTH = NA_HEADS * NA_HEAD_DIM
NA_ROWS = 8
NA_COLS = 16
RET_HEADS = 4
RET_QK_DIM = 64
RET_V_DIM = 2 * RET_QK_DIM
RET_QK_WIDTH = RET_HEADS * RET_QK_DIM
RET_V_WIDTH = RET_HEADS * RET_V_DIM
RET_CHUNK = 128
ROPE_BASE = 10000.0
MIX_WIDTH = NA_WIDTH + RET_V_WIDTH
IN_WIDTH = 3 * NA_WIDTH + 2 * RET_QK_WIDTH + 2 * RET_V_WIDTH
D_FF = 4 * D_MODEL
EPS = 1e-6
NEG = -1e30

kernel_name = 'hybrid_natten_retention_encoder'

F32 = jnp.float32


def rms_norm(x, gain):
    x32 = x.astype(F32)
    y = x32 * lax.rsqrt(jnp.mean(jnp.square(x32), axis=-1, keepdims=True) + EPS)
    return (y * gain.astype(F32)).astype(x.dtype)


def head_group_norm(y, gain):
    bsz, seq, heads, dv = y.shape
    mu = jnp.mean(y, axis=-1, keepdims=True)
    var = jnp.mean(jnp.square(y - mu), axis=-1, keepdims=True)
    y = (y - mu) * lax.rsqrt(var + EPS)
    return y.reshape(bsz, seq, heads * dv) * gain.astype(F32)


def rotary(x, pos):
    d = x.shape[-1]
    inv = 1.0 / (ROPE_BASE ** (jnp.arange(0, d, 2, dtype=F32) / d))
    ang = pos.astype(F32)[:, None] * inv[None, :]
    cos = jnp.cos(ang)[None, :, None, :]
    sin = jnp.sin(ang)[None, :, None, :]
    x1, x2 = x[..., : d // 2], x[..., d // 2:]
    return jnp.concatenate([x1 * cos - x2 * sin, x1 * sin + x2 * cos], axis=-1)


def neighbourhood_attention(q, k, v, rpb):
    bsz, seq, heads, dh = q.shape
    rows = seq // GRID_W
    kh = min(NA_ROWS, rows)
    n_cb = GRID_W // NA_COLS
    band = 2 * NA_COLS
    col = np.arange(GRID_W)
    col_start = np.clip(col - NA_COLS // 2, 0, GRID_W - NA_COLS)
    band_start = np.clip(np.arange(n_cb) * NA_COLS - NA_COLS // 2, 0, GRID_W - band)
    band_idx = band_start[:, None] + np.arange(band)
    q_col = col.reshape(n_cb, NA_COLS)
    lo = col_start[q_col][..., None]
    kcol = band_idx[:, None, :]
    valid = jnp.asarray((kcol >= lo) & (kcol < lo + NA_COLS))
    dc = np.clip(kcol - q_col[..., None], -(NA_COLS - 1), NA_COLS - 1) + NA_COLS - 1
    bias_c = rpb.astype(F32)[:, :, dc]

    kb = k.reshape(bsz, rows, GRID_W, heads, dh)[:, :, band_idx]
    vb = v.reshape(bsz, rows, GRID_W, heads, dh)[:, :, band_idx]
    q_rows = jnp.moveaxis(q.reshape(bsz, rows, n_cb, NA_COLS, heads, dh), 1, 0)
    scale = dh ** -0.5

    def one_row(args):
        q_row, r = args
        rs = jnp.clip(r - kh // 2, 0, rows - kh)
        k_win = lax.dynamic_slice_in_dim(kb, rs, kh, axis=1)
        v_win = lax.dynamic_slice_in_dim(vb, rs, kh, axis=1)
        s = jnp.einsum('bcqhd,bicjhd->bhcqij', q_row, k_win).astype(F32) * scale
        dr = rs - r + jnp.arange(kh) + NA_ROWS - 1
        bias = jnp.take(bias_c, dr, axis=1)
        s = s + jnp.transpose(bias, (0, 2, 3, 1, 4))[None]
        s = jnp.where(valid[None, None, :, :, None, :], s, NEG)
        p = jax.nn.softmax(s.reshape(s.shape[:4] + (kh * band,)), axis=-1)
        p = p.reshape(s.shape).astype(v.dtype)
        return jnp.einsum('bhcqij,bicjhd->bcqhd', p, v_win)

    out = lax.map(one_row, (q_rows, jnp.arange(rows)))
    return jnp.moveaxis(out, 0, 1).reshape(bsz, seq, heads * dh)


def retention_scan(q, k, v, log_gamma, include_diag):
    bsz, seq, heads, dk = q.shape
    dv = v.shape[-1]
    c = RET_CHUNK
    n = seq // c
    qc = q.reshape(bsz, n, c, heads, dk)
    kc = k.reshape(bsz, n, c, heads, dk)
    vc = v.reshape(bsz, n, c, heads, dv)
    idx = jnp.arange(c, dtype=F32)
    diff = idx[:, None] - idx[None, :]
    keep = (diff >= 0) if include_diag else (diff > 0)
    dmat = jnp.where(keep[None], jnp.exp(log_gamma[:, None, None] * jnp.maximum(diff, 0.0)[None]), 0.0)
    s = jnp.einsum('bnihd,bnjhd->bnhij', qc, kc) * dmat
    intra = jnp.einsum('bnhij,bnjhe->bnihe', s, vc)
    zeta = jnp.exp(log_gamma[:, None] * (c - 1 - idx)[None, :])
    chunk_kv = jnp.einsum('bnjhd,hj,bnjhe->nbhde', kc, zeta, vc)
    chunk_decay = jnp.exp(log_gamma * c)[None, :, None, None]

    def step(state, kv):
        return chunk_decay * state + kv, state

    _, prev = lax.scan(step, jnp.zeros((bsz, heads, dk, dv), F32), chunk_kv)
    xi = jnp.exp(log_gamma[:, None] * (idx + 1.0)[None, :])
    cross = jnp.einsum('bnihd,nbhde,hi->bnihe', qc, prev, xi)
    return (intra + cross).reshape(bsz, seq, heads, dv)


def bidirectional_retention(q, k, v, lg_fwd, lg_bwd):
    flip = lambda t: jnp.flip(t, axis=1)
    fwd = retention_scan(q, k, v, lg_fwd, True)
    bwd = flip(retention_scan(flip(q), flip(k), flip(v), lg_bwd, False))
    return fwd + bwd


def log_decay(z):
    return jnp.log1p(-jnp.exp(z.astype(F32)))


def setup_inputs(seed: int = 0) -> dict:
    key = jax.random.key(seed)
    ks = jax.random.split(key, 12)
    base_decay = -(5.0 + jnp.arange(RET_HEADS, dtype=F32)) * float(np.log(2.0))
    x = jax.random.normal(ks[0], (BATCH, SEQ, D_MODEL), F32)
    w_in = jax.random.normal(ks[1], (DEPTH, D_MODEL, IN_WIDTH), F32) * D_MODEL ** -0.5
    w_out = jax.random.normal(ks[2], (DEPTH, MIX_WIDTH, D_MODEL), F32) * MIX_WIDTH ** -0.5
    na_rpb = jax.random.normal(ks[3], (DEPTH, NA_HEADS, 2 * NA_ROWS - 1, 2 * NA_COLS - 1), F32) * 0.02
    ret_decay_fwd = base_decay[None] + 0.05 * jax.random.normal(ks[4], (DEPTH, RET_HEADS), F32)
    ret_decay_bwd = base_decay[None] + 0.05 * jax.random.normal(ks[5], (DEPTH, RET_HEADS), F32)
    ret_norm_gain = 1.0 + 0.02 * jax.random.normal(ks[6], (DEPTH, RET_V_WIDTH), F32)
    norm_mix = 1.0 + 0.02 * jax.random.normal(ks[7], (DEPTH, D_MODEL), F32)
    norm_mlp = 1.0 + 0.02 * jax.random.normal(ks[8], (DEPTH, D_MODEL), F32)
    w_up = jax.random.normal(ks[9], (DEPTH, D_MODEL, D_FF), F32) * D_MODEL ** -0.5
    w_down = jax.random.normal(ks[10], (DEPTH, D_FF, D_MODEL), F32) * D_FF ** -0.5
    norm_final = 1.0 + 0.02 * jax.random.normal(ks[11], (D_MODEL,), F32)
    return {'x': x, 'w_in': w_in, 'w_out': w_out, 'na_rpb': na_rpb,
            'ret_decay_fwd': ret_decay_fwd, 'ret_decay_bwd': ret_decay_bwd,
            'ret_norm_gain': ret_norm_gain, 'norm_mix': norm_mix, 'norm_mlp': norm_mlp,
            'w_up': w_up, 'w_down': w_down, 'norm_final': norm_final}


def reference(x, w_in, w_out, na_rpb, ret_decay_fwd, ret_decay_bwd, ret_norm_gain,
              norm_mix, norm_mlp, w_up, w_down, norm_final):
    bsz, seq, _ = x.shape
    pos = jnp.arange(seq)
    splits = np.cumsum([NA_WIDTH, NA_WIDTH, NA_WIDTH, RET_QK_WIDTH, RET_QK_WIDTH, RET_V_WIDTH])
    for l in range(DEPTH):
        h = rms_norm(x, norm_mix[l])
        proj = h @ w_in[l]
        na_q, na_k, na_v, r_q, r_k, r_v, r_g = jnp.split(proj, splits, axis=-1)
        na_out = neighbourhood_attention(na_q.reshape(bsz, seq, NA_HEADS, NA_HEAD_DIM),
                                         na_k.reshape(bsz, seq, NA_HEADS, NA_HEAD_DIM),
                                         na_v.reshape(bsz, seq, NA_HEADS, NA_HEAD_DIM),
                                         na_rpb[l])
        rq = rotary(r_q.reshape(bsz, seq, RET_HEADS, RET_QK_DIM).astype(F32), pos)
        rk = rotary(r_k.reshape(bsz, seq, RET_HEADS, RET_QK_DIM).astype(F32), pos) * RET_QK_DIM ** -0.5
        rv = r_v.reshape(bsz, seq, RET_HEADS, RET_V_DIM).astype(F32)
        ret = bidirectional_retention(rq, rk, rv, log_decay(ret_decay_fwd[l]), log_decay(ret_decay_bwd[l]))
        ret = head_group_norm(ret, ret_norm_gain[l])
        ret_out = (jax.nn.silu(r_g.astype(F32)) * ret).astype(x.dtype)
        mix = jnp.concatenate([na_out.astype(x.dtype), ret_out], axis=-1)
        x = x + mix @ w_out[l]
        h = rms_norm(x, norm_mlp[l])
        x = x + jnp.square(jax.nn.relu(h @ w_up[l])) @ w_down[l]
    return rms_norm(x, norm_final)
```

```python
import functools

import jax
import jax.numpy as jnp
import numpy as np
from jax import lax
from jax.experimental import pallas as pl
from jax.experimental.pallas import tpu as pltpu

F32 = jnp.float32
BF16 = jnp.bfloat16

GRID_W = 64
NA_HEADS = 8
NA_HEAD_DIM = 64
NA_WIDTH = NA_HEADS * NA_HEAD_DIM
NA_ROWS = 8
NA_COLS = 16
RET_HEADS = 4
RET_QK_DIM = 64
RET_V_DIM = 128
RET_QK_WIDTH = RET_HEADS * RET_QK_DIM
RET_V_WIDTH = RET_HEADS * RET_V_DIM
RET_CHUNK = 128
ROPE_BASE = 10000.0
EPS = 1e-6
NEG = -1e30

LANES = 128
HEAD_PAIRS_NA = NA_WIDTH // LANES
HEAD_PAIRS_RET = RET_QK_WIDTH // LANES
VMEM_LIMIT = 56 * 1024 * 1024

TM_PROJ = 512
TM_MLP = 512
FF_CHUNK = 1024
NA_ROWS_PER_STEP = 8
RET_CHUNKS_PER_STEP = 4

_NT = (((1,), (1,)), ((), ()))
_TN = (((0,), (0,)), ((), ()))


def _rms(x, gain):
    return x * lax.rsqrt(jnp.mean(x * x, axis=-1, keepdims=True) + EPS) * gain


def _lane_is_second_head():
    return lax.broadcasted_iota(jnp.int32, (1, LANES), 1) >= NA_HEAD_DIM


def _in_proj_kernel(x_ref, gain_ref, w_ref, cos_ref, sin_ref,
                    q_ref, k_ref, v_ref, rq_ref, rk_ref, rv_ref, rg_ref):
    h = _rms(x_ref[...], gain_ref[...]).astype(BF16)

    def proj(c0, c1):
        return jnp.dot(h, w_ref[:, c0:c1], preferred_element_type=F32)

    o = 0
    q_ref[...] = (proj(o, o + NA_WIDTH) * NA_HEAD_DIM ** -0.5).astype(BF16)
    o += NA_WIDTH
    k_ref[...] = proj(o, o + NA_WIDTH).astype(BF16)
    o += NA_WIDTH
    v_ref[...] = proj(o, o + NA_WIDTH).astype(BF16)
    o += NA_WIDTH

    cos = cos_ref[...]
    sin = sin_ref[...]
    lane = lax.broadcasted_iota(jnp.int32, (1, LANES), 1)
    first_half = (lane % RET_QK_DIM) < RET_QK_DIM // 2

    def rotary(t, scale):
        parts = []
        for p in range(HEAD_PAIRS_RET):
            tp = t[:, p * LANES:(p + 1) * LANES]
            swapped = jnp.where(first_half,
                                pltpu.roll(tp, LANES - RET_QK_DIM // 2, 1),
                                pltpu.roll(tp, RET_QK_DIM // 2, 1))
            parts.append((tp * cos + swapped * sin) * scale)
        return jnp.concatenate(parts, axis=-1).astype(BF16)

    rq_ref[...] = rotary(proj(o, o + RET_QK_WIDTH), 1.0)
    o += RET_QK_WIDTH
    rk_ref[...] = rotary(proj(o, o + RET_QK_WIDTH), RET_QK_DIM ** -0.5)
    o += RET_QK_WIDTH
    rv_ref[...] = proj(o, o + RET_V_WIDTH).astype(BF16)
    o += RET_V_WIDTH
    rg_ref[...] = proj(o, o + RET_V_WIDTH).astype(BF16)


def _in_proj(x2, gain, w, cos_t, sin_t, seq):
    n, d = x2.shape
    tm = TM_PROJ
    pos_blocks = seq // tm
    tok = lambda i: (i, 0)
    const = lambda i: (0, 0)
    widths = (NA_WIDTH, NA_WIDTH, NA_WIDTH, RET_QK_WIDTH, RET_QK_WIDTH, RET_V_WIDTH, RET_V_WIDTH)
    return pl.pallas_call(
        _in_proj_kernel,
        out_shape=[jax.ShapeDtypeStruct((n, w_), BF16) for w_ in widths],
        grid=(n // tm,),
        in_specs=[
            pl.BlockSpec((tm, d), tok),
            pl.BlockSpec((1, d), const),
            pl.BlockSpec(w.shape, const, pipeline_mode=pl.Buffered(1)),
            pl.BlockSpec((tm, LANES), lambda i: (i % pos_blocks, 0)),
            pl.BlockSpec((tm, LANES), lambda i: (i % pos_blocks, 0)),
        ],
        out_specs=[pl.BlockSpec((tm, w_), tok) for w_ in widths],
        compiler_params=pltpu.CompilerParams(
            dimension_semantics=("parallel",), vmem_limit_bytes=VMEM_LIMIT),
        name="in_proj",
    )(x2, gain, w, cos_t, sin_t)


def _natten_kernel(q_ref, k_ref, v_ref, bias_ref, o_ref, *, rows):
    r0 = pl.program_id(1) * NA_ROWS_PER_STEP
    second = _lane_is_second_head()
    keys = NA_ROWS * GRID_W

    def row_body(rr, carry):
        r = r0 + rr
        rs = jnp.clip(r - NA_ROWS // 2, 0, rows - NA_ROWS)
        d0 = rs - r + NA_ROWS - 1
        qoff = pl.multiple_of(rr * GRID_W, GRID_W)
        koff = pl.multiple_of(rs * GRID_W, GRID_W)
        for p in range(HEAD_PAIRS_NA):
            cols = slice(p * LANES, (p + 1) * LANES)
            qp = q_ref[0, pl.ds(qoff, GRID_W), cols]
            kp = k_ref[0, pl.ds(koff, keys), cols]
            vp = v_ref[0, pl.ds(koff, keys), cols]
            out = None
            for hh in range(2):
                mask = second if hh else jnp.logical_not(second)
                qm = jnp.where(mask, qp, jnp.zeros_like(qp))
                s = lax.dot_general(qm, kp, _NT, preferred_element_type=F32)
                bias = jnp.concatenate(
                    [bias_ref[2 * p + hh, d0 + 2 * j] for j in range(NA_ROWS // 2)], axis=-1)
                s = s + bias
                e = jnp.exp(s - jnp.max(s, axis=-1, keepdims=True))
                denom = jnp.sum(e, axis=-1, keepdims=True)
                o = jnp.dot(e.astype(BF16), vp, preferred_element_type=F32) / denom
                out = o if hh == 0 else jnp.where(mask, o, out)
            o_ref[0, pl.ds(qoff, GRID_W), cols] = out.astype(BF16)
        return carry

    lax.fori_loop(0, NA_ROWS_PER_STEP, row_body, 0)


def _natten_bias_table(rpb):
    qc = np.arange(GRID_W)[:, None]
    kc = np.arange(GRID_W)[None, :]
    dc = np.clip(kc - qc, -(NA_COLS - 1), NA_COLS - 1) + NA_COLS - 1
    lo = np.clip(qc - NA_COLS // 2, 0, GRID_W - NA_COLS)
    valid = (kc >= lo) & (kc < lo + NA_COLS)
    t = jnp.where(jnp.asarray(valid), rpb.astype(F32)[:, :, dc], NEG)
    return jnp.concatenate([t[:, :-1], t[:, 1:]], axis=-1)


def _natten(q, k, v, bias):
    b, s, w = q.shape
    rows = s // GRID_W
    tq = NA_ROWS_PER_STEP * GRID_W
    return pl.pallas_call(
        functools.partial(_natten_kernel, rows=rows),
        out_shape=jax.ShapeDtypeStruct((b, s, w), BF16),
        grid=(b, rows // NA_ROWS_PER_STEP),
        in_specs=[
            pl.BlockSpec((1, tq, w), lambda i, j: (i, j, 0)),
            pl.BlockSpec((1, s, w), lambda i, j: (i, 0, 0)),
            pl.BlockSpec((1, s, w), lambda i, j: (i, 0, 0)),
            pl.BlockSpec(bias.shape, lambda i, j: (0, 0, 0, 0), pipeline_mode=pl.Buffered(1)),
        ],
        out_specs=pl.BlockSpec((1, tq, w), lambda i, j: (i, j, 0)),
        compiler_params=pltpu.CompilerParams(
            dimension_semantics=("parallel", "arbitrary"), vmem_limit_bytes=VMEM_LIMIT),
        name="natten",
    )(q, k, v, bias)


def _pair_lane_values(smem_ref, p):
    return jnp.where(_lane_is_second_head(), smem_ref[2 * p + 1], smem_ref[2 * p])


def _pair_row_values(smem_ref, p):
    row = lax.broadcasted_iota(jnp.int32, (LANES, 1), 0)
    return jnp.where(row >= RET_QK_DIM, smem_ref[2 * p + 1], smem_ref[2 * p])


def _chunk_kv(k_scaled, v_ref, rows, p):
    second = _lane_is_second_head()
    upd = None
    for hh in range(2):
        mask = second if hh else jnp.logical_not(second)
        km = jnp.where(mask, k_scaled, 0.0).astype(BF16)
        h = 2 * p + hh
        vh = v_ref[0, rows, h * RET_V_DIM:(h + 1) * RET_V_DIM]
        t = lax.dot_general(km, vh, _TN, preferred_element_type=F32)
        upd = t if upd is None else upd + t
    return upd


def _ret_bwd_kernel(lgb_ref, k_ref, v_ref, sb_ref, state_ref):
    @pl.when(pl.program_id(1) == 0)
    def _():
        state_ref[...] = jnp.zeros_like(state_ref)

    c = RET_CHUNK
    pos = lax.broadcasted_iota(jnp.int32, (c, 1), 0).astype(F32)
    for cc in reversed(range(RET_CHUNKS_PER_STEP)):
        rows = slice(cc * c, (cc + 1) * c)
        for p in range(HEAD_PAIRS_RET):
            cols = slice(p * LANES, (p + 1) * LANES)
            sb_ref[0, cc, p] = state_ref[p].astype(BF16)
            zeta = jnp.exp(_pair_lane_values(lgb_ref, p) * pos)
            upd = _chunk_kv(k_ref[0, rows, cols].astype(F32) * zeta, v_ref, rows, p)
            decay = jnp.exp(_pair_row_values(lgb_ref, p) * float(c))
            state_ref[p] = decay * state_ref[p] + upd


def _ret_bwd(lgb, rk, rv):
    b, s, _ = rk.shape
    tc = RET_CHUNKS_PER_STEP * RET_CHUNK
    steps = s // tc
    rev = lambda i, j: (i, steps - 1 - j, 0)
    return pl.pallas_call(
        _ret_bwd_kernel,
        out_shape=jax.ShapeDtypeStruct((b, s // RET_CHUNK, HEAD_PAIRS_RET, LANES, RET_V_DIM), BF16),
        grid=(b, steps),
        in_specs=[
            pl.BlockSpec(memory_space=pltpu.SMEM),
            pl.BlockSpec((1, tc, RET_QK_WIDTH), rev),
            pl.BlockSpec((1, tc, RET_V_WIDTH), rev),
        ],
        out_specs=pl.BlockSpec((1, RET_CHUNKS_PER_STEP, HEAD_PAIRS_RET, LANES, RET_V_DIM),
                               lambda i, j: (i, steps - 1 - j, 0, 0, 0)),
        scratch_shapes=[pltpu.VMEM((HEAD_PAIRS_RET, LANES, RET_V_DIM), F32)],
        compiler_params=pltpu.CompilerParams(
            dimension_semantics=("parallel", "arbitrary"), vmem_limit_bytes=VMEM_LIMIT),
        name="ret_bwd",
    )(lgb, rk, rv)


def _ret_fwd_kernel(lgf_ref, lgb_ref, q_ref, k_ref, v_ref, g_ref, sb_ref, gain_ref,
                    o_ref, state_ref, dmat_ref, tab_ref):
    c = RET_CHUNK

    @pl.when(pl.program_id(1) == 0)
    def _():
        state_ref[...] = jnp.zeros_like(state_ref)
        i = lax.broadcasted_iota(jnp.int32, (c, c), 0)
        j = lax.broadcasted_iota(jnp.int32, (c, c), 1)
        diff = (i - j).astype(F32)
        for h in range(RET_HEADS):
            dmat_ref[h] = jnp.exp(jnp.where(diff >= 0, lgf_ref[h] * diff, -lgb_ref[h] * diff))
        pos = lax.broadcasted_iota(jnp.int32, (c, 1), 0).astype(F32)
        for p in range(HEAD_PAIRS_RET):
            lf = _pair_lane_values(lgf_ref, p)
            lb = _pair_lane_values(lgb_ref, p)
            tab_ref[p, 0] = jnp.exp(lf * (pos + 1.0))
            tab_ref[p, 1] = jnp.exp(lb * (float(c) - pos))
            tab_ref[p, 2] = jnp.exp(lf * (float(c) - 1.0 - pos))

    second = _lane_is_second_head()
    for cc in range(RET_CHUNKS_PER_STEP):
        rows = slice(cc * c, (cc + 1) * c)
        for p in range(HEAD_PAIRS_RET):
            cols = slice(p * LANES, (p + 1) * LANES)
            qp = q_ref[0, rows, cols]
            kp = k_ref[0, rows, cols]
            q32 = qp.astype(F32)
            qf = q32 * tab_ref[p, 0]
            qb = q32 * tab_ref[p, 1]
            sf = state_ref[p].astype(BF16)
            sb = sb_ref[0, cc, p]
            for hh in range(2):
                h = 2 * p + hh
                vcols = slice(h * RET_V_DIM, (h + 1) * RET_V_DIM)
                mask = second if hh else jnp.logical_not(second)
                qm = jnp.where(mask, qp, jnp.zeros_like(qp))
                s = lax.dot_general(qm, kp, _NT, preferred_element_type=F32) * dmat_ref[h]
                y = jnp.dot(s.astype(BF16), v_ref[0, rows, vcols], preferred_element_type=F32)
                y += jnp.dot(jnp.where(mask, qf, 0.0).astype(BF16), sf, preferred_element_type=F32)
                y += jnp.dot(jnp.where(mask, qb, 0.0).astype(BF16), sb, preferred_element_type=F32)
                mu = jnp.mean(y, axis=-1, keepdims=True)
                yc = y - mu
                var = jnp.mean(yc * yc, axis=-1, keepdims=True)
                yn = yc * lax.rsqrt(var + EPS) * gain_ref[:, vcols]
                g = g_ref[0, rows, vcols].astype(F32)
                o_ref[0, rows, vcols] = (g / (1.0 + jnp.exp(-g)) * yn).astype(BF16)
            upd = _chunk_kv(kp.astype(F32) * tab_ref[p, 2], v_ref, rows, p)
            decay = jnp.exp(_pair_row_values(lgf_ref, p) * float(c))
            state_ref[p] = decay * state_ref[p] + upd


def _ret_fwd(lgf, lgb, rq, rk, rv, rg, sb, gain):
    b, s, _ = rq.shape
    tc = RET_CHUNKS_PER_STEP * RET_CHUNK
    tok = lambda i, j: (i, j, 0)
    smem = pl.BlockSpec(memory_space=pltpu.SMEM)
    return pl.pallas_call(
        _ret_fwd_kernel,
        out_shape=jax.ShapeDtypeStruct((b, s, RET_V_WIDTH), BF16),
        grid=(b, s // tc),
        in_specs=[
            smem, smem,
            pl.BlockSpec((1, tc, RET_QK_WIDTH), tok),
            pl.BlockSpec((1, tc, RET_QK_WIDTH), tok),
            pl.BlockSpec((1, tc, RET_V_WIDTH), tok),
            pl.BlockSpec((1, tc, RET_V_WIDTH), tok),
            pl.BlockSpec((1, RET_CHUNKS_PER_STEP, HEAD_PAIRS_RET, LANES, RET_V_DIM),
                         lambda i, j: (i, j, 0, 0, 0)),
            pl.BlockSpec((1, RET_V_WIDTH), lambda i, j: (0, 0)),
        ],
        out_specs=pl.BlockSpec((1, tc, RET_V_WIDTH), tok),
        scratch_shapes=[
            pltpu.VMEM((HEAD_PAIRS_RET, LANES, RET_V_DIM), F32),
            pltpu.VMEM((RET_HEADS, RET_CHUNK, RET_CHUNK), F32),
            pltpu.VMEM((HEAD_PAIRS_RET, 3, RET_CHUNK, LANES), F32),
        ],
        compiler_params=pltpu.CompilerParams(
            dimension_semantics=("arbitrary", "arbitrary"), vmem_limit_bytes=VMEM_LIMIT),
        name="ret_fwd",
    )(lgf, lgb, rq, rk, rv, rg, sb, gain)


def _out_mlp_kernel(x_ref, na_ref, ret_ref, wo_ref, gain_ref, wu_ref, wd_ref, gfin_ref, o_ref,
                    *, final_norm):
    mix = (jnp.dot(na_ref[...], wo_ref[:NA_WIDTH, :], preferred_element_type=F32)
           + jnp.dot(ret_ref[...], wo_ref[NA_WIDTH:, :], preferred_element_type=F32))
    x1 = x_ref[...] + mix
    h = _rms(x1, gain_ref[...]).astype(BF16)
    mlp = None
    d_ff = wu_ref.shape[1]
    for f0 in range(0, d_ff, FF_CHUNK):
        u = jnp.maximum(jnp.dot(h, wu_ref[:, f0:f0 + FF_CHUNK], preferred_element_type=F32), 0.0)
        t = jnp.dot((u * u).astype(BF16), wd_ref[f0:f0 + FF_CHUNK, :], preferred_element_type=F32)
        mlp = t if mlp is None else mlp + t
    x3 = x1 + mlp
    o_ref[...] = _rms(x3, gfin_ref[...]) if final_norm else x3


def _out_mlp(x2, na, ret, wo, gain, wu, wd, gfin, final_norm):
    n, d = x2.shape
    tm = TM_MLP
    tok = lambda i: (i, 0)
    const = lambda i: (0, 0)
    resident = lambda a: pl.BlockSpec(a.shape, const, pipeline_mode=pl.Buffered(1))
    return pl.pallas_call(
        functools.partial(_out_mlp_kernel, final_norm=final_norm),
        out_shape=jax.ShapeDtypeStruct((n, d), F32),
        grid=(n // tm,),
        in_specs=[
            pl.BlockSpec((tm, d), tok),
            pl.BlockSpec((tm, NA_WIDTH), tok),
            pl.BlockSpec((tm, RET_V_WIDTH), tok),
            resident(wo),
            pl.BlockSpec((1, d), const),
            resident(wu),
            resident(wd),
            pl.BlockSpec((1, d), const),
        ],
        out_specs=pl.BlockSpec((tm, d), tok),
        compiler_params=pltpu.CompilerParams(
            dimension_semantics=("parallel",), vmem_limit_bytes=VMEM_LIMIT),
        name="out_mlp",
    )(x2, na, ret, wo, gain, wu, wd, gfin)


def _rotary_tables(seq):
    half = RET_QK_DIM // 2
    inv = 1.0 / (ROPE_BASE ** (jnp.arange(0, RET_QK_DIM, 2, dtype=F32) / RET_QK_DIM))
    ang = jnp.arange(seq).astype(F32)[:, None] * inv[None, :]
    cos = jnp.cos(ang)
    sin = jnp.sin(ang)
    cos_t = jnp.tile(cos, (1, LANES // half))
    sin_t = jnp.tile(jnp.concatenate([-sin, sin], axis=-1), (1, LANES // RET_QK_DIM))
    return cos_t, sin_t


def kernel(x, w_in, w_out, na_rpb, ret_decay_fwd, ret_decay_bwd, ret_norm_gain,
           norm_mix, norm_mlp, w_up, w_down, norm_final):
    bsz, seq, d = x.shape
    depth = w_in.shape[0]
    n = bsz * seq
    assert seq % TM_PROJ == 0 and n % TM_MLP == 0
    assert seq % (RET_CHUNKS_PER_STEP * RET_CHUNK) == 0
    assert (seq // GRID_W) % NA_ROWS_PER_STEP == 0 and seq // GRID_W >= NA_ROWS

    cos_t, sin_t = _rotary_tables(seq)
    x2 = x.reshape(n, d)
    for l in range(depth):
        q, k, v, rq, rk, rv, rg = _in_proj(
            x2, norm_mix[l].reshape(1, d), w_in[l].astype(BF16), cos_t, sin_t, seq)
        as_seq = lambda t: t.reshape(bsz, seq, t.shape[-1])
        na = _natten(as_seq(q), as_seq(k), as_seq(v), _natten_bias_table(na_rpb[l]))
        lgf = jnp.log1p(-jnp.exp(ret_decay_fwd[l].astype(F32)))
        lgb = jnp.log1p(-jnp.exp(ret_decay_bwd[l].astype(F32)))
        sb = _ret_bwd(lgb, as_seq(rk), as_seq(rv))
        ret = _ret_fwd(lgf, lgb, as_seq(rq), as_seq(rk), as_seq(rv), as_seq(rg), sb,
                       ret_norm_gain[l].reshape(1, RET_V_WIDTH).astype(F32))
        x2 = _out_mlp(x2, na.reshape(n, NA_WIDTH), ret.reshape(n, RET_V_WIDTH),
                      w_out[l].astype(BF16), norm_mlp[l].reshape(1, d),
                      w_up[l].astype(BF16), w_down[l].astype(BF16),
                      norm_final.reshape(1, d), l == depth - 1)
    return x2.reshape(bsz, seq, d)
```

```python
import functools

import jax
import jax.numpy as jnp
import numpy as np
from jax import lax
from jax.experimental import pallas as pl
from jax.experimental.pallas import tpu as pltpu

F32 = jnp.float32
BF16 = jnp.bfloat16

GRID_W = 64
NA_HEADS = 8
NA_HEAD_DIM = 64
NA_WIDTH = NA_HEADS * NA_HEAD_DIM
NA_ROWS = 8
NA_COLS = 16
RET_HEADS = 4
RET_QK_DIM = 64
RET_V_DIM = 128
RET_QK_WIDTH = RET_HEADS * RET_QK_DIM
RET_V_WIDTH = RET_HEADS * RET_V_DIM
RET_CHUNK = 128
ROPE_BASE = 10000.0
EPS = 1e-6
NEG = -1e30

LANES = 128
HEAD_PAIRS_NA = NA_WIDTH // LANES
HEAD_PAIRS_RET = RET_QK_WIDTH // LANES
VMEM_LIMIT = 56 * 1024 * 1024

TM_PROJ = 1024
TM_MLP = 512
FF_CHUNK = 1024
NA_ROWS_PER_STEP = 8
RET_CHUNKS_PER_STEP = 4

_NT = (((1,), (1,)), ((), ()))
_TN = (((0,), (0,)), ((), ()))


def _rms(x, gain):
    return x * lax.rsqrt(jnp.mean(x * x, axis=-1, keepdims=True) + EPS) * gain


def _lane_is_second_head():
    return lax.broadcasted_iota(jnp.int32, (1, LANES), 1) >= NA_HEAD_DIM


def _in_proj_kernel(lgf_ref, lgb_ref, x_ref, gain_ref, w_ref, cos_ref, sin_ref,
                    q_ref, k_ref, v_ref, rq_ref, rk_ref, qf_ref, qb_ref, kf_ref, kb_ref, rv_ref, rg_ref,
                    decay_ref):
    c = RET_CHUNK

    @pl.when(pl.program_id(0) == 0)
    def _():
        pos = lax.broadcasted_iota(jnp.int32, (c, 1), 0).astype(F32)
        for p in range(HEAD_PAIRS_RET):
            lanes = slice(p * LANES, (p + 1) * LANES)
            lf = _pair_lane_values(lgf_ref, p)
            lb = _pair_lane_values(lgb_ref, p)
            decay_ref[0, :, lanes] = jnp.exp(lf * (pos + 1.0))
            decay_ref[1, :, lanes] = jnp.exp(lb * (float(c) - pos))
            decay_ref[2, :, lanes] = jnp.exp(lf * (float(c) - 1.0 - pos))
            decay_ref[3, :, lanes] = jnp.exp(lb * pos)

    h = _rms(x_ref[...], gain_ref[...]).astype(BF16)
    tm = h.shape[0]

    def decayed(t, which):
        t3 = t.reshape(tm // c, c, RET_QK_WIDTH) * decay_ref[which][None]
        return t3.reshape(tm, RET_QK_WIDTH).astype(BF16)

    def proj(c0, c1):
        return jnp.dot(h, w_ref[:, c0:c1], preferred_element_type=F32)

    starts = np.cumsum([0, NA_WIDTH, NA_WIDTH, NA_WIDTH, RET_QK_WIDTH, RET_QK_WIDTH, RET_V_WIDTH])
    c_q, c_k, c_v, c_rq, c_rk, c_rv, c_rg = (int(s) for s in starts)

    cos = cos_ref[...]
    sin = sin_ref[...]
    lane = lax.broadcasted_iota(jnp.int32, (1, LANES), 1)
    first_half = (lane % RET_QK_DIM) < RET_QK_DIM // 2

    def rotary(t, scale):
        parts = []
        for p in range(HEAD_PAIRS_RET):
            tp = t[:, p * LANES:(p + 1) * LANES]
            swapped = jnp.where(first_half,
                                pltpu.roll(tp, LANES - RET_QK_DIM // 2, 1),
                                pltpu.roll(tp, RET_QK_DIM // 2, 1))
            parts.append((tp * cos + swapped * sin) * scale)
        return jnp.concatenate(parts, axis=-1)

    rq = rotary(proj(c_rq, c_rq + RET_QK_WIDTH), 1.0)
    rq_ref[...] = rq.astype(BF16)
    qf_ref[...] = decayed(rq, 0)
    qb_ref[...] = decayed(rq, 1)
    rk = rotary(proj(c_rk, c_rk + RET_QK_WIDTH), RET_QK_DIM ** -0.5)
    rk_ref[...] = rk.astype(BF16)
    kf_ref[...] = decayed(rk, 2)
    kb_ref[...] = decayed(rk, 3)
    g = proj(c_rg, c_rg + RET_V_WIDTH)
    rg_ref[...] = (g / (1.0 + jnp.exp(-g))).astype(BF16)
    rv_ref[...] = proj(c_rv, c_rv + RET_V_WIDTH).astype(BF16)
    q_ref[...] = (proj(c_q, c_q + NA_WIDTH) * NA_HEAD_DIM ** -0.5).astype(BF16)
    k_ref[...] = proj(c_k, c_k + NA_WIDTH).astype(BF16)
    v_ref[...] = proj(c_v, c_v + NA_WIDTH).astype(BF16)


def _in_proj(lgf, lgb, x2, gain, w, cos_t, sin_t, seq):
    n, d = x2.shape
    tm = TM_PROJ
    pos_blocks = seq // tm
    tok = lambda i: (i, 0)
    const = lambda i: (0, 0)
    smem = pl.BlockSpec(memory_space=pltpu.SMEM)
    widths = (NA_WIDTH,) * 3 + (RET_QK_WIDTH,) * 6 + (RET_V_WIDTH,) * 2
    return pl.pallas_call(
        _in_proj_kernel,
        out_shape=[jax.ShapeDtypeStruct((n, w_), BF16) for w_ in widths],
        grid=(n // tm,),
        in_specs=[
            smem, smem,
            pl.BlockSpec((tm, d), tok),
            pl.BlockSpec((1, d), const),
            pl.BlockSpec(w.shape, const, pipeline_mode=pl.Buffered(1)),
            pl.BlockSpec((tm, LANES), lambda i: (i % pos_blocks, 0)),
            pl.BlockSpec((tm, LANES), lambda i: (i % pos_blocks, 0)),
        ],
        out_specs=[pl.BlockSpec((tm, w_), tok) for w_ in widths],
        scratch_shapes=[pltpu.VMEM((4, RET_CHUNK, RET_QK_WIDTH), F32)],
        compiler_params=pltpu.CompilerParams(
            dimension_semantics=("arbitrary",), vmem_limit_bytes=VMEM_LIMIT),
        name="in_proj",
    )(lgf, lgb, x2, gain, w, cos_t, sin_t)


def _natten_kernel(q_ref, k_ref, v_ref, bias_ref, o_ref, *, rows):
    r0 = pl.program_id(1) * NA_ROWS_PER_STEP
    second = _lane_is_second_head()
    keys = NA_ROWS * GRID_W

    def row_body(rr, carry):
        r = r0 + rr
        rs = jnp.clip(r - NA_ROWS // 2, 0, rows - NA_ROWS)
        d0 = rs - r + NA_ROWS - 1
        qoff = pl.multiple_of(rr * GRID_W, GRID_W)
        koff = pl.multiple_of(rs * GRID_W, GRID_W)
        pairs = range(HEAD_PAIRS_NA)
        cols = [slice(p * LANES, (p + 1) * LANES) for p in pairs]
        scores = []
        for p in pairs:
            qp = q_ref[0, pl.ds(qoff, GRID_W), cols[p]]
            q2 = jnp.concatenate([jnp.where(second, jnp.zeros_like(qp), qp),
                                  jnp.where(second, qp, jnp.zeros_like(qp))], axis=0)
            kp = k_ref[0, pl.ds(koff, keys), cols[p]]
            s = lax.dot_general(q2, kp, _NT, preferred_element_type=F32)
            bias = jnp.concatenate(
                [bias_ref[p, d0 + 2 * j] for j in range(NA_ROWS // 2)], axis=-1)
            scores.append(s + bias)
        probs, denoms = [], []
        for p in pairs:
            s = scores[p]
            e = jnp.exp(s - jnp.max(s, axis=-1, keepdims=True))
            denoms.append(jnp.sum(e, axis=-1, keepdims=True))
            probs.append(e.astype(BF16))
        for p in pairs:
            vp = v_ref[0, pl.ds(koff, keys), cols[p]]
            o2 = jnp.dot(probs[p], vp, preferred_element_type=F32) / denoms[p]
            out = jnp.where(second, o2[GRID_W:], o2[:GRID_W])
            o_ref[0, pl.ds(qoff, GRID_W), cols[p]] = out.astype(BF16)
        return carry

    lax.fori_loop(0, NA_ROWS_PER_STEP, row_body, 0, unroll=2)


def _natten_bias_table(rpb):
    qc = np.arange(GRID_W)[:, None]
    kc = np.arange(GRID_W)[None, :]
    dc = np.clip(kc - qc, -(NA_COLS - 1), NA_COLS - 1) + NA_COLS - 1
    lo = np.clip(qc - NA_COLS // 2, 0, GRID_W - NA_COLS)
    valid = (kc >= lo) & (kc < lo + NA_COLS)
    onehot = (dc[None] == np.arange(2 * NA_COLS - 1)[:, None, None]).astype(np.float32)
    t = jnp.einsum('hdc,cqk->hdqk', rpb.astype(F32), jnp.asarray(onehot),
                   precision=lax.Precision.HIGHEST)
    t = jnp.where(jnp.asarray(valid), t, NEG)
    t2 = jnp.concatenate([t[:, :-1], t[:, 1:]], axis=-1)
    t2 = t2.reshape(HEAD_PAIRS_NA, 2, 2 * NA_ROWS - 2, GRID_W, 2 * GRID_W)
    return jnp.transpose(t2, (0, 2, 1, 3, 4)).reshape(
        HEAD_PAIRS_NA, 2 * NA_ROWS - 2, 2 * GRID_W, 2 * GRID_W)


def _natten(q, k, v, bias):
    b, s, w = q.shape
    rows = s // GRID_W
    tq = NA_ROWS_PER_STEP * GRID_W
    return pl.pallas_call(
        functools.partial(_natten_kernel, rows=rows),
        out_shape=jax.ShapeDtypeStruct((b, s, w), BF16),
        grid=(b, rows // NA_ROWS_PER_STEP),
        in_specs=[
            pl.BlockSpec((1, tq, w), lambda i, j: (i, j, 0)),
            pl.BlockSpec((1, s, w), lambda i, j: (i, 0, 0)),
            pl.BlockSpec((1, s, w), lambda i, j: (i, 0, 0)),
            pl.BlockSpec(bias.shape, lambda i, j: (0, 0, 0, 0), pipeline_mode=pl.Buffered(1)),
        ],
        out_specs=pl.BlockSpec((1, tq, w), lambda i, j: (i, j, 0)),
        compiler_params=pltpu.CompilerParams(
            dimension_semantics=("parallel", "arbitrary"), vmem_limit_bytes=VMEM_LIMIT),
        name="natten",
    )(q, k, v, bias)


def _pair_lane_values(smem_ref, p):
    return jnp.where(_lane_is_second_head(), smem_ref[2 * p + 1], smem_ref[2 * p])


def _pair_row_values(smem_ref, p):
    row = lax.broadcasted_iota(jnp.int32, (LANES, 1), 0)
    return jnp.where(row >= RET_QK_DIM, smem_ref[2 * p + 1], smem_ref[2 * p])


def _second_head_rows():
    return lax.broadcasted_iota(jnp.int32, (LANES, 1), 0) >= RET_QK_DIM


def _chunk_kv(k_decayed, v_ref, rows, p):
    v2 = v_ref[0, rows, 2 * p * RET_V_DIM:(2 * p + 2) * RET_V_DIM]
    t = lax.dot_general(k_decayed, v2, _TN, preferred_element_type=F32)
    return jnp.where(_second_head_rows(), t[:, RET_V_DIM:], t[:, :RET_V_DIM])


def _per_head_states(state):
    second = _second_head_rows()
    return (jnp.where(second, 0.0, state).astype(BF16), jnp.where(second, state, 0.0).astype(BF16))


def _ret_bwd_kernel(lgb_ref, kb_ref, v_ref, sb_ref, state_ref):
    @pl.when(pl.program_id(1) == 0)
    def _():
        state_ref[...] = jnp.zeros_like(state_ref)

    c = RET_CHUNK
    chunks = range(RET_CHUNKS_PER_STEP)
    pairs = range(HEAD_PAIRS_RET)
    upd = {}
    for p in pairs:
        for cc in chunks:
            rows = slice(cc * c, (cc + 1) * c)
            upd[cc, p] = _chunk_kv(kb_ref[0, rows, p * LANES:(p + 1) * LANES], v_ref, rows, p)
    for p in pairs:
        decay = jnp.exp(_pair_row_values(lgb_ref, p) * float(c))
        state = state_ref[p]
        for cc in reversed(chunks):
            sb_ref[0, cc, 2 * p], sb_ref[0, cc, 2 * p + 1] = _per_head_states(state)
            state = decay * state + upd[cc, p]
        state_ref[p] = state


def _ret_bwd(lgb, kb, rv):
    b, s, _ = kb.shape
    tc = RET_CHUNKS_PER_STEP * RET_CHUNK
    steps = s // tc
    rev = lambda i, j: (i, steps - 1 - j, 0)
    return pl.pallas_call(
        _ret_bwd_kernel,
        out_shape=jax.ShapeDtypeStruct((b, s // RET_CHUNK, RET_HEADS, LANES, RET_V_DIM), BF16),
        grid=(b, steps),
        in_specs=[
            pl.BlockSpec(memory_space=pltpu.SMEM),
            pl.BlockSpec((1, tc, RET_QK_WIDTH), rev),
            pl.BlockSpec((1, tc, RET_V_WIDTH), rev),
        ],
        out_specs=pl.BlockSpec((1, RET_CHUNKS_PER_STEP, RET_HEADS, LANES, RET_V_DIM),
                               lambda i, j: (i, steps - 1 - j, 0, 0, 0)),
        scratch_shapes=[pltpu.VMEM((HEAD_PAIRS_RET, LANES, RET_V_DIM), F32)],
        compiler_params=pltpu.CompilerParams(
            dimension_semantics=("parallel", "arbitrary"), vmem_limit_bytes=VMEM_LIMIT),
        name="ret_bwd",
    )(lgb, kb, rv)


def _ret_fwd_kernel(lgf_ref, lgb_ref, q_ref, k_ref, qf_ref, qb_ref, kf_ref, v_ref, gate_ref, sb_ref,
                    gain_ref, o_ref, state_ref, dmat_ref):
    c = RET_CHUNK

    @pl.when(pl.program_id(1) == 0)
    def _():
        state_ref[...] = jnp.zeros_like(state_ref)
        i = lax.broadcasted_iota(jnp.int32, (c, c), 0)
        j = lax.broadcasted_iota(jnp.int32, (c, c), 1)
        diff = (i - j).astype(F32)
        for h in range(RET_HEADS):
            dmat_ref[h // 2, (h % 2) * c:(h % 2 + 1) * c, :] = jnp.exp(
                jnp.where(diff >= 0, lgf_ref[h] * diff, -lgb_ref[h] * diff))

    second = _lane_is_second_head()
    chunks = range(RET_CHUNKS_PER_STEP)
    pairs = range(HEAD_PAIRS_RET)
    rows = [slice(cc * c, (cc + 1) * c) for cc in chunks]
    cols = [slice(p * LANES, (p + 1) * LANES) for p in pairs]

    upd = {(cc, p): _chunk_kv(kf_ref[0, rows[cc], cols[p]], v_ref, rows[cc], p)
           for cc in chunks for p in pairs}

    scores = {}
    for cc in chunks:
        for p in pairs:
            qp = q_ref[0, rows[cc], cols[p]]
            q2 = jnp.concatenate([jnp.where(second, jnp.zeros_like(qp), qp),
                                  jnp.where(second, qp, jnp.zeros_like(qp))], axis=0)
            s = lax.dot_general(q2, k_ref[0, rows[cc], cols[p]], _NT, preferred_element_type=F32)
            scores[cc, p] = (s * dmat_ref[p]).astype(BF16)

    before = {}
    for p in pairs:
        decay = jnp.exp(_pair_row_values(lgf_ref, p) * float(c))
        state = state_ref[p]
        for cc in chunks:
            before[cc, 2 * p], before[cc, 2 * p + 1] = _per_head_states(state)
            state = decay * state + upd[cc, p]
        state_ref[p] = state

    for cc in chunks:
        for p in pairs:
            qf = qf_ref[0, rows[cc], cols[p]]
            qb = qb_ref[0, rows[cc], cols[p]]
            for hh in range(2):
                h = 2 * p + hh
                vcols = slice(h * RET_V_DIM, (h + 1) * RET_V_DIM)
                lhs = jnp.concatenate([scores[cc, p][hh * c:(hh + 1) * c], qf, qb], axis=-1)
                rhs = jnp.concatenate([v_ref[0, rows[cc], vcols], before[cc, h], sb_ref[0, cc, h]], axis=0)
                y = jnp.dot(lhs, rhs, preferred_element_type=F32)
                mu = jnp.mean(y, axis=-1, keepdims=True)
                yc = y - mu
                var = jnp.mean(yc * yc, axis=-1, keepdims=True)
                yn = yc * lax.rsqrt(var + EPS) * gain_ref[:, vcols]
                o_ref[0, rows[cc], vcols] = (gate_ref[0, rows[cc], vcols].astype(F32) * yn).astype(BF16)


def _ret_fwd(lgf, lgb, rq, rk, qf, qb, kf, rv, gate, sb, gain):
    b, s, _ = rq.shape
    tc = RET_CHUNKS_PER_STEP * RET_CHUNK
    tok = lambda i, j: (i, j, 0)
    smem = pl.BlockSpec(memory_space=pltpu.SMEM)
    qk_spec = pl.BlockSpec((1, tc, RET_QK_WIDTH), tok)
    v_spec = pl.BlockSpec((1, tc, RET_V_WIDTH), tok)
    return pl.pallas_call(
        _ret_fwd_kernel,
        out_shape=jax.ShapeDtypeStruct((b, s, RET_V_WIDTH), BF16),
        grid=(b, s // tc),
        in_specs=[
            smem, smem,
            qk_spec, qk_spec, qk_spec, qk_spec, qk_spec, v_spec, v_spec,
            pl.BlockSpec((1, RET_CHUNKS_PER_STEP, RET_HEADS, LANES, RET_V_DIM),
                         lambda i, j: (i, j, 0, 0, 0)),
            pl.BlockSpec((1, RET_V_WIDTH), lambda i, j: (0, 0)),
        ],
        out_specs=pl.BlockSpec((1, tc, RET_V_WIDTH), tok),
        scratch_shapes=[
            pltpu.VMEM((HEAD_PAIRS_RET, LANES, RET_V_DIM), F32),
            pltpu.VMEM((HEAD_PAIRS_RET, 2 * RET_CHUNK, RET_CHUNK), F32),
        ],
        compiler_params=pltpu.CompilerParams(
            dimension_semantics=("arbitrary", "arbitrary"), vmem_limit_bytes=VMEM_LIMIT),
        name="ret_fwd",
    )(lgf, lgb, rq, rk, qf, qb, kf, rv, gate, sb, gain)


def _out_mlp_kernel(x_ref, na_ref, ret_ref, wo_ref, gain_ref, wu_ref, wd_ref, gfin_ref, o_ref,
                    *, final_norm):
    mix = (jnp.dot(na_ref[...], wo_ref[:NA_WIDTH, :], preferred_element_type=F32)
           + jnp.dot(ret_ref[...], wo_ref[NA_WIDTH:, :], preferred_element_type=F32))
    x1 = x_ref[...] + mix
    h = _rms(x1, gain_ref[...]).astype(BF16)
    mlp = None
    d_ff = wu_ref.shape[1]
    for f0 in range(0, d_ff, FF_CHUNK):
        u = jnp.maximum(jnp.dot(h, wu_ref[:, f0:f0 + FF_CHUNK], preferred_element_type=F32), 0.0)
        t = jnp.dot((u * u).astype(BF16), wd_ref[f0:f0 + FF_CHUNK, :], preferred_element_type=F32)
        mlp = t if mlp is None else mlp + t
    x3 = x1 + mlp
    o_ref[...] = _rms(x3, gfin_ref[...]) if final_norm else x3


def _out_mlp(x2, na, ret, wo, gain, wu, wd, gfin, final_norm):
    n, d = x2.shape
    tm = TM_MLP
    tok = lambda i: (i, 0)
    const = lambda i: (0, 0)
    resident = lambda a: pl.BlockSpec(a.shape, const, pipeline_mode=pl.Buffered(1))
    return pl.pallas_call(
        functools.partial(_out_mlp_kernel, final_norm=final_norm),
        out_shape=jax.ShapeDtypeStruct((n, d), F32),
        grid=(n // tm,),
        in_specs=[
            pl.BlockSpec((tm, d), tok),
            pl.BlockSpec((tm, NA_WIDTH), tok),
            pl.BlockSpec((tm, RET_V_WIDTH), tok),
            resident(wo),
            pl.BlockSpec((1, d), const),
            resident(wu),
            resident(wd),
            pl.BlockSpec((1, d), const),
        ],
        out_specs=pl.BlockSpec((tm, d), tok),
        compiler_params=pltpu.CompilerParams(
            dimension_semantics=("parallel",), vmem_limit_bytes=VMEM_LIMIT),
        name="out_mlp",
    )(x2, na, ret, wo, gain, wu, wd, gfin)


def _rotary_tables(seq):
    half = RET_QK_DIM // 2
    inv = 1.0 / (ROPE_BASE ** (jnp.arange(0, RET_QK_DIM, 2, dtype=F32) / RET_QK_DIM))
    ang = jnp.arange(seq).astype(F32)[:, None] * inv[None, :]
    cos = jnp.cos(ang)
    sin = jnp.sin(ang)
    cos_t = jnp.tile(cos, (1, LANES // half))
    sin_t = jnp.tile(jnp.concatenate([-sin, sin], axis=-1), (1, LANES // RET_QK_DIM))
    return cos_t, sin_t


def kernel(x, w_in, w_out, na_rpb, ret_decay_fwd, ret_decay_bwd, ret_norm_gain,
           norm_mix, norm_mlp, w_up, w_down, norm_final):
    bsz, seq, d = x.shape
    depth = w_in.shape[0]
    n = bsz * seq
    assert seq % TM_PROJ == 0 and n % TM_MLP == 0
    assert seq % (RET_CHUNKS_PER_STEP * RET_CHUNK) == 0
    assert (seq // GRID_W) % NA_ROWS_PER_STEP == 0 and seq // GRID_W >= NA_ROWS

    cos_t, sin_t = _rotary_tables(seq)
    x2 = x.reshape(n, d)
    for l in range(depth):
        lgf = jnp.log1p(-jnp.exp(ret_decay_fwd[l].astype(F32)))
        lgb = jnp.log1p(-jnp.exp(ret_decay_bwd[l].astype(F32)))
        q, k, v, rq, rk, qf, qb, kf, kb, rv, gate = _in_proj(
            lgf, lgb, x2, norm_mix[l].reshape(1, d), w_in[l].astype(BF16), cos_t, sin_t, seq)
        as_seq = lambda t: t.reshape(bsz, seq, t.shape[-1])
        na = _natten(as_seq(q), as_seq(k), as_seq(v), _natten_bias_table(na_rpb[l]))
        sb = _ret_bwd(lgb, as_seq(kb), as_seq(rv))
        ret = _ret_fwd(lgf, lgb, as_seq(rq), as_seq(rk), as_seq(qf), as_seq(qb), as_seq(kf),
                       as_seq(rv), as_seq(gate), sb,
                       ret_norm_gain[l].reshape(1, RET_V_WIDTH).astype(F32))
        x2 = _out_mlp(x2, na.reshape(n, NA_WIDTH), ret.reshape(n, RET_V_WIDTH),
                      w_out[l].astype(BF16), norm_mlp[l].reshape(1, d),
                      w_up[l].astype(BF16), w_down[l].astype(BF16),
                      norm_final.reshape(1, d), l == depth - 1)
    return x2.reshape(bsz, seq, d)
```

```python
import functools

import jax
import jax.numpy as jnp
import numpy as np
from jax import lax
from jax.experimental import pallas as pl
from jax.experimental.pallas import tpu as pltpu

F32 = jnp.float32
BF16 = jnp.bfloat16

GRID_W = 64
NA_HEADS = 8
NA_HEAD_DIM = 64
NA_WIDTH = NA_HEADS * NA_HEAD_DIM
NA_ROWS = 8
NA_COLS = 16
RET_HEADS = 4
RET_QK_DIM = 64
RET_V_DIM = 128
RET_QK_WIDTH = RET_HEADS * RET_QK_DIM
RET_V_WIDTH = RET_HEADS * RET_V_DIM
RET_CHUNK = 128
ROPE_BASE = 10000.0
EPS = 1e-6
NEG = -1e30

LANES = 128
HEAD_PAIRS_NA = NA_WIDTH // LANES
HEAD_PAIRS_RET = RET_QK_WIDTH // LANES
VMEM_LIMIT = 56 * 1024 * 1024

TM_PROJ = 1024
TM_MLP = 1024
FF_CHUNK = 1024
NA_ROWS_PER_STEP = 8
NA_ROW_UNROLL = 4
RET_BWD_CHUNKS_PER_STEP = 16
RET_FWD_CHUNKS_PER_STEP = 8

_NT = (((1,), (1,)), ((), ()))
_TN = (((0,), (0,)), ((), ()))


def _rms(x, gain):
    return x * lax.rsqrt(jnp.mean(x * x, axis=-1, keepdims=True) + EPS) * gain


def _lane_is_second_head():
    return lax.broadcasted_iota(jnp.int32, (1, LANES), 1) >= NA_HEAD_DIM


def _pair_lane_values(smem_ref, layer, p):
    return jnp.where(_lane_is_second_head(), smem_ref[layer, 2 * p + 1], smem_ref[layer, 2 * p])


def _second_head_rows():
    return lax.broadcasted_iota(jnp.int32, (LANES, 1), 0) >= RET_QK_DIM


def _pair_row_values(smem_ref, layer, p):
    return jnp.where(_second_head_rows(), smem_ref[layer, 2 * p + 1], smem_ref[layer, 2 * p])


def _layer_block(a, layer, grid_rank, **kw):
    zeros = (0,) * (a.ndim - 1)
    if grid_rank == 1:
        index_map = lambda i: (layer,) + zeros
    else:
        index_map = lambda i, j: (layer,) + zeros
    return pl.BlockSpec((None,) + a.shape[1:], index_map, **kw)


def _in_proj_kernel(lgf_ref, lgb_ref, x_ref, gain_ref, w_ref, cos_ref, sin_ref,
                    q_ref, k_ref, v_ref, rq_ref, rk_ref, qf_ref, qb_ref, kf_ref, kb_ref, rv_ref, rg_ref,
                    decay_ref, *, layer):
    c = RET_CHUNK

    @pl.when(pl.program_id(0) == 0)
    def _():
        pos = lax.broadcasted_iota(jnp.int32, (c, 1), 0).astype(F32)
        for p in range(HEAD_PAIRS_RET):
            lanes = slice(p * LANES, (p + 1) * LANES)
            lf = _pair_lane_values(lgf_ref, layer, p)
            lb = _pair_lane_values(lgb_ref, layer, p)
            decay_ref[0, :, lanes] = jnp.exp(lf * (pos + 1.0))
            decay_ref[1, :, lanes] = jnp.exp(lb * (float(c) - pos))
            decay_ref[2, :, lanes] = jnp.exp(lf * (float(c) - 1.0 - pos))
            decay_ref[3, :, lanes] = jnp.exp(lb * pos)

    h = _rms(x_ref[...], gain_ref[...]).astype(BF16)
    tm = h.shape[0]

    def decayed(t, which):
        t3 = t.reshape(tm // c, c, RET_QK_WIDTH) * decay_ref[which][None]
        return t3.reshape(tm, RET_QK_WIDTH).astype(BF16)

    def proj(c0, c1):
        return jnp.dot(h, w_ref[:, c0:c1], preferred_element_type=F32)

    starts = np.cumsum([0, NA_WIDTH, NA_WIDTH, NA_WIDTH, RET_QK_WIDTH, RET_QK_WIDTH, RET_V_WIDTH])
    c_q, c_k, c_v, c_rq, c_rk, c_rv, c_rg = (int(s) for s in starts)

    cos = cos_ref[...]
    sin = sin_ref[...]
    lane = lax.broadcasted_iota(jnp.int32, (1, LANES), 1)
    first_half = (lane % RET_QK_DIM) < RET_QK_DIM // 2

    def rotary(t, scale):
        parts = []
        for p in range(HEAD_PAIRS_RET):
            tp = t[:, p * LANES:(p + 1) * LANES]
            swapped = jnp.where(first_half,
                                pltpu.roll(tp, LANES - RET_QK_DIM // 2, 1),
                                pltpu.roll(tp, RET_QK_DIM // 2, 1))
            parts.append((tp * cos + swapped * sin) * scale)
        return jnp.concatenate(parts, axis=-1)

    rq = rotary(proj(c_rq, c_rq + RET_QK_WIDTH), 1.0)
    rq_ref[...] = rq.astype(BF16)
    qf_ref[...] = decayed(rq, 0)
    qb_ref[...] = decayed(rq, 1)
    rk = rotary(proj(c_rk, c_rk + RET_QK_WIDTH), RET_QK_DIM ** -0.5)
    rk_ref[...] = rk.astype(BF16)
    kf_ref[...] = decayed(rk, 2)
    kb_ref[...] = decayed(rk, 3)
    g = proj(c_rg, c_rg + RET_V_WIDTH)
    rg_ref[...] = (g / (1.0 + jnp.exp(-g))).astype(BF16)
    rv_ref[...] = proj(c_rv, c_rv + RET_V_WIDTH).astype(BF16)
    q_ref[...] = (proj(c_q, c_q + NA_WIDTH) * NA_HEAD_DIM ** -0.5).astype(BF16)
    k_ref[...] = proj(c_k, c_k + NA_WIDTH).astype(BF16)
    v_ref[...] = proj(c_v, c_v + NA_WIDTH).astype(BF16)


def _in_proj(layer, lgf, lgb, x2, gains, w, cos_t, sin_t, seq):
    n, d = x2.shape
    tm = TM_PROJ
    pos_blocks = seq // tm
    tok = lambda i: (i, 0)
    smem = pl.BlockSpec(memory_space=pltpu.SMEM)
    widths = (NA_WIDTH,) * 3 + (RET_QK_WIDTH,) * 6 + (RET_V_WIDTH,) * 2
    return pl.pallas_call(
        functools.partial(_in_proj_kernel, layer=layer),
        out_shape=[jax.ShapeDtypeStruct((n, w_), BF16) for w_ in widths],
        grid=(n // tm,),
        in_specs=[
            smem, smem,
            pl.BlockSpec((tm, d), tok),
            _layer_block(gains, layer, 1),
            _layer_block(w, layer, 1, pipeline_mode=pl.Buffered(1)),
            pl.BlockSpec((tm, LANES), lambda i: (i % pos_blocks, 0)),
            pl.BlockSpec((tm, LANES), lambda i: (i % pos_blocks, 0)),
        ],
        out_specs=[pl.BlockSpec((tm, w_), tok) for w_ in widths],
        scratch_shapes=[pltpu.VMEM((4, RET_CHUNK, RET_QK_WIDTH), F32)],
        compiler_params=pltpu.CompilerParams(
            dimension_semantics=("arbitrary",), vmem_limit_bytes=VMEM_LIMIT),
        name="in_proj",
    )(lgf, lgb, x2, gains, w, cos_t, sin_t)


def _natten_kernel(q_ref, k_ref, v_ref, bias_ref, o_ref, *, rows):
    second = _lane_is_second_head()
    pairs = range(HEAD_PAIRS_NA)
    cols = [slice(p * LANES, (p + 1) * LANES) for p in pairs]

    def stack_heads(qp):
        return jnp.concatenate([jnp.where(second, jnp.zeros_like(qp), qp),
                                jnp.where(second, qp, jnp.zeros_like(qp))], axis=0)

    def softmax_terms(s):
        e = jnp.exp(s - jnp.max(s, axis=-1, keepdims=True))
        return e.astype(BF16), jnp.sum(e, axis=-1, keepdims=True)

    keys = NA_ROWS * GRID_W
    r0 = pl.program_id(1) * NA_ROWS_PER_STEP

    def row_body(rr, carry):
        r = r0 + rr
        rs = jnp.clip(r - NA_ROWS // 2, 0, rows - NA_ROWS)
        d0 = rs - r + NA_ROWS - 1
        qoff = pl.multiple_of(rr * GRID_W, GRID_W)
        koff = pl.multiple_of(rs * GRID_W, GRID_W)
        scores = []
        for p in pairs:
            q2 = stack_heads(q_ref[0, pl.ds(qoff, GRID_W), cols[p]])
            kp = k_ref[0, pl.ds(koff, keys), cols[p]]
            bias = jnp.concatenate([bias_ref[p, d0 + 2 * j] for j in range(NA_ROWS // 2)], axis=-1)
            scores.append(lax.dot_general(q2, kp, _NT, preferred_element_type=F32) + bias)
        terms = [softmax_terms(s) for s in scores]
        for p in pairs:
            e, denom = terms[p]
            vp = v_ref[0, pl.ds(koff, keys), cols[p]]
            o2 = jnp.dot(e, vp, preferred_element_type=F32) / denom
            out = jnp.where(second, o2[GRID_W:], o2[:GRID_W])
            o_ref[0, pl.ds(qoff, GRID_W), cols[p]] = out.astype(BF16)
        return carry

    lax.fori_loop(0, NA_ROWS_PER_STEP, row_body, 0, unroll=NA_ROW_UNROLL)


def _natten_bias_table(rpb):
    depth = rpb.shape[0]
    qc = np.arange(GRID_W)[:, None]
    kc = np.arange(GRID_W)[None, :]
    dc = np.clip(kc - qc, -(NA_COLS - 1), NA_COLS - 1) + NA_COLS - 1
    lo = np.clip(qc - NA_COLS // 2, 0, GRID_W - NA_COLS)
    valid = (kc >= lo) & (kc < lo + NA_COLS)
    onehot = (dc[None] == np.arange(2 * NA_COLS - 1)[:, None, None]).astype(np.float32)
    t = jnp.einsum('lhdc,cqk->lhdqk', rpb.astype(F32), jnp.asarray(onehot),
                   precision=lax.Precision.HIGHEST)
    t = jnp.where(jnp.asarray(valid), t, NEG)

    def row_pairs(u):
        return jnp.concatenate([u[:, :, :-1], u[:, :, 1:]], axis=-1)

    n_off = 2 * NA_ROWS - 2
    t2 = row_pairs(t).reshape(depth, HEAD_PAIRS_NA, 2, n_off, GRID_W, 2 * GRID_W)
    return jnp.transpose(t2, (0, 1, 3, 2, 4, 5)).reshape(
        depth, HEAD_PAIRS_NA, n_off, 2 * GRID_W, 2 * GRID_W)


def _natten(layer, q, k, v, bias):
    b, s, w = q.shape
    rows = s // GRID_W
    tq = NA_ROWS_PER_STEP * GRID_W
    return pl.pallas_call(
        functools.partial(_natten_kernel, rows=rows),
        out_shape=jax.ShapeDtypeStruct((b, s, w), BF16),
        grid=(b, rows // NA_ROWS_PER_STEP),
        in_specs=[
            pl.BlockSpec((1, tq, w), lambda i, j: (i, j, 0)),
            pl.BlockSpec((1, s, w), lambda i, j: (i, 0, 0)),
            pl.BlockSpec((1, s, w), lambda i, j: (i, 0, 0)),
            _layer_block(bias, layer, 2, pipeline_mode=pl.Buffered(1)),
        ],
        out_specs=pl.BlockSpec((1, tq, w), lambda i, j: (i, j, 0)),
        compiler_params=pltpu.CompilerParams(
            dimension_semantics=("parallel", "arbitrary"), vmem_limit_bytes=VMEM_LIMIT),
        name="natten",
    )(q, k, v, bias)


def _chunk_kv(k_decayed, v_ref, rows, p):
    v2 = v_ref[0, rows, 2 * p * RET_V_DIM:(2 * p + 2) * RET_V_DIM]
    t = lax.dot_general(k_decayed, v2, _TN, preferred_element_type=F32)
    return jnp.where(_second_head_rows(), t[:, RET_V_DIM:], t[:, :RET_V_DIM])


def _per_head_states(state):
    second = _second_head_rows()
    return (jnp.where(second, 0.0, state).astype(BF16), jnp.where(second, state, 0.0).astype(BF16))


def _ret_bwd_kernel(lgb_ref, kb_ref, v_ref, sb_ref, state_ref, *, layer):
    @pl.when(pl.program_id(1) == 0)
    def _():
        state_ref[...] = jnp.zeros_like(state_ref)

    c = RET_CHUNK
    chunks = range(RET_BWD_CHUNKS_PER_STEP)
    pairs = range(HEAD_PAIRS_RET)
    upd = {}
    for p in pairs:
        for cc in chunks:
            rows = slice(cc * c, (cc + 1) * c)
            upd[cc, p] = _chunk_kv(kb_ref[0, rows, p * LANES:(p + 1) * LANES], v_ref, rows, p)
    for p in pairs:
        decay = jnp.exp(_pair_row_values(lgb_ref, layer, p) * float(c))
        state = state_ref[p]
        for cc in reversed(chunks):
            sb_ref[0, cc, 2 * p], sb_ref[0, cc, 2 * p + 1] = _per_head_states(state)
            state = decay * state + upd[cc, p]
        state_ref[p] = state


def _ret_bwd(layer, lgb, kb, rv):
    b, s, _ = kb.shape
    nch = RET_BWD_CHUNKS_PER_STEP
    tc = nch * RET_CHUNK
    steps = s // tc
    rev = lambda i, j: (i, steps - 1 - j, 0)
    return pl.pallas_call(
        functools.partial(_ret_bwd_kernel, layer=layer),
        out_shape=jax.ShapeDtypeStruct((b, s // RET_CHUNK, RET_HEADS, LANES, RET_V_DIM), BF16),
        grid=(b, steps),
        in_specs=[
            pl.BlockSpec(memory_space=pltpu.SMEM),
            pl.BlockSpec((1, tc, RET_QK_WIDTH), rev),
            pl.BlockSpec((1, tc, RET_V_WIDTH), rev),
        ],
        out_specs=pl.BlockSpec((1, nch, RET_HEADS, LANES, RET_V_DIM),
                               lambda i, j: (i, steps - 1 - j, 0, 0, 0)),
        scratch_shapes=[pltpu.VMEM((HEAD_PAIRS_RET, LANES, RET_V_DIM), F32)],
        compiler_params=pltpu.CompilerParams(
            dimension_semantics=("parallel", "arbitrary"), vmem_limit_bytes=VMEM_LIMIT),
        name="ret_bwd",
    )(lgb, kb, rv)


def _ret_fwd_kernel(lgf_ref, lgb_ref, q_ref, k_ref, qf_ref, qb_ref, kf_ref, v_ref, gate_ref, sb_ref,
                    gain_ref, o_ref, state_ref, dmat_ref, *, layer):
    c = RET_CHUNK

    @pl.when(pl.program_id(1) == 0)
    def _():
        state_ref[...] = jnp.zeros_like(state_ref)
        i = lax.broadcasted_iota(jnp.int32, (c, c), 0)
        j = lax.broadcasted_iota(jnp.int32, (c, c), 1)
        diff = (i - j).astype(F32)
        for h in range(RET_HEADS):
            dmat_ref[h // 2, (h % 2) * c:(h % 2 + 1) * c, :] = jnp.exp(
                jnp.where(diff >= 0, lgf_ref[layer, h] * diff, -lgb_ref[layer, h] * diff))

    second = _lane_is_second_head()
    chunks = range(RET_FWD_CHUNKS_PER_STEP)
    pairs = range(HEAD_PAIRS_RET)
    rows = [slice(cc * c, (cc + 1) * c) for cc in chunks]
    cols = [slice(p * LANES, (p + 1) * LANES) for p in pairs]

    upd = {(cc, p): _chunk_kv(kf_ref[0, rows[cc], cols[p]], v_ref, rows[cc], p)
           for cc in chunks for p in pairs}

    scores = {}
    for cc in chunks:
        for p in pairs:
            qp = q_ref[0, rows[cc], cols[p]]
            q2 = jnp.concatenate([jnp.where(second, jnp.zeros_like(qp), qp),
                                  jnp.where(second, qp, jnp.zeros_like(qp))], axis=0)
            s = lax.dot_general(q2, k_ref[0, rows[cc], cols[p]], _NT, preferred_element_type=F32)
            scores[cc, p] = (s * dmat_ref[p]).astype(BF16)

    before = {}
    for p in pairs:
        decay = jnp.exp(_pair_row_values(lgf_ref, layer, p) * float(c))
        state = state_ref[p]
        for cc in chunks:
            before[cc, 2 * p], before[cc, 2 * p + 1] = _per_head_states(state)
            state = decay * state + upd[cc, p]
        state_ref[p] = state

    for cc in chunks:
        for p in pairs:
            qf = qf_ref[0, rows[cc], cols[p]]
            qb = qb_ref[0, rows[cc], cols[p]]
            for hh in range(2):
                h = 2 * p + hh
                vcols = slice(h * RET_V_DIM, (h + 1) * RET_V_DIM)
                lhs = jnp.concatenate([scores[cc, p][hh * c:(hh + 1) * c], qf, qb], axis=-1)
                rhs = jnp.concatenate([v_ref[0, rows[cc], vcols], before[cc, h], sb_ref[0, cc, h]], axis=0)
                y = jnp.dot(lhs, rhs, preferred_element_type=F32)
                mu = jnp.mean(y, axis=-1, keepdims=True)
                yc = y - mu
                var = jnp.mean(yc * yc, axis=-1, keepdims=True)
                yn = yc * lax.rsqrt(var + EPS) * gain_ref[:, vcols]
                o_ref[0, rows[cc], vcols] = (gate_ref[0, rows[cc], vcols].astype(F32) * yn).astype(BF16)


def _ret_fwd(layer, lgf, lgb, rq, rk, qf, qb, kf, rv, gate, sb, gains):
    b, s, _ = rq.shape
    nch = RET_FWD_CHUNKS_PER_STEP
    tc = nch * RET_CHUNK
    tok = lambda i, j: (i, j, 0)
    smem = pl.BlockSpec(memory_space=pltpu.SMEM)
    qk_spec = pl.BlockSpec((1, tc, RET_QK_WIDTH), tok)
    v_spec = pl.BlockSpec((1, tc, RET_V_WIDTH), tok)
    return pl.pallas_call(
        functools.partial(_ret_fwd_kernel, layer=layer),
        out_shape=jax.ShapeDtypeStruct((b, s, RET_V_WIDTH), BF16),
        grid=(b, s // tc),
        in_specs=[
            smem, smem,
            qk_spec, qk_spec, qk_spec, qk_spec, qk_spec, v_spec, v_spec,
            pl.BlockSpec((1, nch, RET_HEADS, LANES, RET_V_DIM), lambda i, j: (i, j, 0, 0, 0)),
            _layer_block(gains, layer, 2),
        ],
        out_specs=pl.BlockSpec((1, tc, RET_V_WIDTH), tok),
        scratch_shapes=[
            pltpu.VMEM((HEAD_PAIRS_RET, LANES, RET_V_DIM), F32),
            pltpu.VMEM((HEAD_PAIRS_RET, 2 * RET_CHUNK, RET_CHUNK), F32),
        ],
        compiler_params=pltpu.CompilerParams(
            dimension_semantics=("arbitrary", "arbitrary"), vmem_limit_bytes=VMEM_LIMIT),
        name="ret_fwd",
    )(lgf, lgb, rq, rk, qf, qb, kf, rv, gate, sb, gains)


def _out_mlp_kernel(x_ref, na_ref, ret_ref, wo_ref, gain_ref, wu_ref, wd_ref, gfin_ref, o_ref,
                    *, final_norm):
    mix = (jnp.dot(na_ref[...], wo_ref[:NA_WIDTH, :], preferred_element_type=F32)
           + jnp.dot(ret_ref[...], wo_ref[NA_WIDTH:, :], preferred_element_type=F32))
    x1 = x_ref[...] + mix
    h = _rms(x1, gain_ref[...]).astype(BF16)
    mlp = None
    d_ff = wu_ref.shape[1]
    for f0 in range(0, d_ff, FF_CHUNK):
        u = jnp.maximum(jnp.dot(h, wu_ref[:, f0:f0 + FF_CHUNK], preferred_element_type=F32), 0.0)
        t = jnp.dot((u * u).astype(BF16), wd_ref[f0:f0 + FF_CHUNK, :], preferred_element_type=F32)
        mlp = t if mlp is None else mlp + t
    x3 = x1 + mlp
    o_ref[...] = _rms(x3, gfin_ref[...]) if final_norm else x3


def _out_mlp(layer, x2, na, ret, wo, gains, wu, wd, gfin, final_norm):
    n, d = x2.shape
    tm = TM_MLP
    tok = lambda i: (i, 0)
    resident = lambda a: _layer_block(a, layer, 1, pipeline_mode=pl.Buffered(1))
    return pl.pallas_call(
        functools.partial(_out_mlp_kernel, final_norm=final_norm),
        out_shape=jax.ShapeDtypeStruct((n, d), F32),
        grid=(n // tm,),
        in_specs=[
            pl.BlockSpec((tm, d), tok),
            pl.BlockSpec((tm, NA_WIDTH), tok),
            pl.BlockSpec((tm, RET_V_WIDTH), tok),
            resident(wo),
            _layer_block(gains, layer, 1),
            resident(wu),
            resident(wd),
            pl.BlockSpec((1, d), lambda i: (0, 0)),
        ],
        out_specs=pl.BlockSpec((tm, d), tok),
        compiler_params=pltpu.CompilerParams(
            dimension_semantics=("parallel",), vmem_limit_bytes=VMEM_LIMIT),
        name="out_mlp",
    )(x2, na, ret, wo, gains, wu, wd, gfin)


def _rotary_tables(seq):
    half = RET_QK_DIM // 2
    inv = 1.0 / (ROPE_BASE ** (jnp.arange(0, RET_QK_DIM, 2, dtype=F32) / RET_QK_DIM))
    ang = jnp.arange(seq).astype(F32)[:, None] * inv[None, :]
    cos = jnp.cos(ang)
    sin = jnp.sin(ang)
    cos_t = jnp.tile(cos, (1, LANES // half))
    sin_t = jnp.tile(jnp.concatenate([-sin, sin], axis=-1), (1, LANES // RET_QK_DIM))
    return cos_t, sin_t


def kernel(x, w_in, w_out, na_rpb, ret_decay_fwd, ret_decay_bwd, ret_norm_gain,
           norm_mix, norm_mlp, w_up, w_down, norm_final):
    bsz, seq, d = x.shape
    depth = w_in.shape[0]
    n = bsz * seq
    rows = seq // GRID_W
    assert TM_PROJ % RET_CHUNK == 0 and seq % TM_PROJ == 0 and n % TM_MLP == 0
    assert seq % (RET_BWD_CHUNKS_PER_STEP * RET_CHUNK) == 0
    assert seq % (RET_FWD_CHUNKS_PER_STEP * RET_CHUNK) == 0
    assert rows % NA_ROWS_PER_STEP == 0 and NA_ROWS_PER_STEP % NA_ROW_UNROLL == 0 and rows >= NA_ROWS

    cos_t, sin_t = _rotary_tables(seq)
    w_in_b, w_out_b, w_up_b, w_down_b = (w.astype(BF16) for w in (w_in, w_out, w_up, w_down))
    lgf = jnp.log1p(-jnp.exp(ret_decay_fwd.astype(F32)))
    lgb = jnp.log1p(-jnp.exp(ret_decay_bwd.astype(F32)))
    bias = _natten_bias_table(na_rpb)
    g_mix = norm_mix.astype(F32).reshape(depth, 1, d)
    g_mlp = norm_mlp.astype(F32).reshape(depth, 1, d)
    g_ret = ret_norm_gain.astype(F32).reshape(depth, 1, RET_V_WIDTH)
    g_fin = norm_final.astype(F32).reshape(1, d)

    x2 = x.reshape(n, d)
    as_seq = lambda t: t.reshape(bsz, seq, t.shape[-1])
    for l in range(depth):
        q, k, v, rq, rk, qf, qb, kf, kb, rv, gate = _in_proj(
            l, lgf, lgb, x2, g_mix, w_in_b, cos_t, sin_t, seq)
        na = _natten(l, as_seq(q), as_seq(k), as_seq(v), bias)
        sb = _ret_bwd(l, lgb, as_seq(kb), as_seq(rv))
        ret = _ret_fwd(l, lgf, lgb, as_seq(rq), as_seq(rk), as_seq(qf), as_seq(qb), as_seq(kf),
                       as_seq(rv), as_seq(gate), sb, g_ret)
        x2 = _out_mlp(l, x2, na.reshape(n, NA_WIDTH), ret.reshape(n, RET_V_WIDTH),
                      w_out_b, g_mlp, w_up_b, w_down_b, g_fin, l == depth - 1)
    return x2.reshape(bsz, seq, d)
```

```python
import functools

import jax
import jax.numpy as jnp
import numpy as np
from jax import lax
from jax.experimental import pallas as pl
from jax.experimental.pallas import tpu as pltpu

F32 = jnp.float32
BF16 = jnp.bfloat16

GRID_W = 64
NA_HEADS = 8
NA_HEAD_DIM = 64
NA_WIDTH = NA_HEADS * NA_HEAD_DIM
NA_ROWS = 8
NA_COLS = 16
RET_HEADS = 4
RET_QK_DIM = 64
RET_V_DIM = 128
RET_QK_WIDTH = RET_HEADS * RET_QK_DIM
RET_V_WIDTH = RET_HEADS * RET_V_DIM
RET_CHUNK = 128
ROPE_BASE = 10000.0
EPS = 1e-6
NEG = -1e30

LANES = 128
HEAD_PAIRS_NA = NA_WIDTH // LANES
HEAD_PAIRS_RET = RET_QK_WIDTH // LANES
VMEM_LIMIT = 60 * 1024 * 1024

TM_PROJ = 1024
FF_CHUNK = 1024
NA_ROWS_PER_STEP = 8
RET_BWD_CHUNKS_PER_STEP = 16
RET_FWD_CHUNKS_PER_STEP = 8

_NT = (((1,), (1,)), ((), ()))
_TN = (((0,), (0,)), ((), ()))


def _rms(x, gain):
    return x * lax.rsqrt(jnp.mean(x * x, axis=-1, keepdims=True) + EPS) * gain


def _lane_is_second_head():
    return lax.broadcasted_iota(jnp.int32, (1, LANES), 1) >= NA_HEAD_DIM


def _pair_lane_values(smem_ref, layer, p):
    return jnp.where(_lane_is_second_head(), smem_ref[layer, 2 * p + 1], smem_ref[layer, 2 * p])


def _second_head_rows():
    return lax.broadcasted_iota(jnp.int32, (LANES, 1), 0) >= RET_QK_DIM


def _pair_row_values(smem_ref, layer, p):
    return jnp.where(_second_head_rows(), smem_ref[layer, 2 * p + 1], smem_ref[layer, 2 * p])


def _layer_block(a, layer, grid_rank, **kw):
    zeros = (0,) * (a.ndim - 1)
    if grid_rank == 1:
        index_map = lambda i: (layer,) + zeros
    else:
        index_map = lambda i, j: (layer,) + zeros
    return pl.BlockSpec((None,) + a.shape[1:], index_map, **kw)


def _in_proj_kernel(lgf_ref, lgb_ref, x_ref, gain_ref, w_ref, cos_ref, sin_ref,
                    q_ref, k_ref, v_ref, rq_ref, rk_ref, qf_ref, qb_ref, kf_ref, kb_ref, rv_ref, rg_ref,
                    decay_ref, *, layer):
    c = RET_CHUNK

    @pl.when(pl.program_id(0) == 0)
    def _():
        pos = lax.broadcasted_iota(jnp.int32, (c, 1), 0).astype(F32)
        for p in range(HEAD_PAIRS_RET):
            lanes = slice(p * LANES, (p + 1) * LANES)
            lf = _pair_lane_values(lgf_ref, layer, p)
            lb = _pair_lane_values(lgb_ref, layer, p)
            decay_ref[0, :, lanes] = jnp.exp(lf * (pos + 1.0))
            decay_ref[1, :, lanes] = jnp.exp(lb * (float(c) - pos))
            decay_ref[2, :, lanes] = jnp.exp(lf * (float(c) - 1.0 - pos))
            decay_ref[3, :, lanes] = jnp.exp(lb * pos)

    h = _rms(x_ref[...], gain_ref[...]).astype(BF16)
    tm = h.shape[0]

    def decayed(t, which):
        t3 = t.reshape(tm // c, c, RET_QK_WIDTH) * decay_ref[which][None]
        return t3.reshape(tm, RET_QK_WIDTH).astype(BF16)

    def proj(c0, c1):
        return jnp.dot(h, w_ref[:, c0:c1], preferred_element_type=F32)

    starts = np.cumsum([0, NA_WIDTH, NA_WIDTH, NA_WIDTH, RET_QK_WIDTH, RET_QK_WIDTH, RET_V_WIDTH])
    c_q, c_k, c_v, c_rq, c_rk, c_rv, c_rg = (int(s) for s in starts)

    cos = cos_ref[...]
    sin = sin_ref[...]
    lane = lax.broadcasted_iota(jnp.int32, (1, LANES), 1)
    first_half = (lane % RET_QK_DIM) < RET_QK_DIM // 2

    def rotary(t, scale):
        parts = []
        for p in range(HEAD_PAIRS_RET):
            tp = t[:, p * LANES:(p + 1) * LANES]
            swapped = jnp.where(first_half,
                                pltpu.roll(tp, LANES - RET_QK_DIM // 2, 1),
                                pltpu.roll(tp, RET_QK_DIM // 2, 1))
            parts.append((tp * cos + swapped * sin) * scale)
        return jnp.concatenate(parts, axis=-1)

    rq = rotary(proj(c_rq, c_rq + RET_QK_WIDTH), 1.0)
    rq_ref[...] = rq.astype(BF16)
    qf_ref[...] = decayed(rq, 0)
    qb_ref[...] = decayed(rq, 1)
    rk = rotary(proj(c_rk, c_rk + RET_QK_WIDTH), RET_QK_DIM ** -0.5)
    rk_ref[...] = rk.astype(BF16)
    kf_ref[...] = decayed(rk, 2)
    kb_ref[...] = decayed(rk, 3)
    g = proj(c_rg, c_rg + RET_V_WIDTH)
    rg_ref[...] = (g / (1.0 + jnp.exp(-g))).astype(BF16)
    rv_ref[...] = proj(c_rv, c_rv + RET_V_WIDTH).astype(BF16)
    q_ref[...] = (proj(c_q, c_q + NA_WIDTH) * NA_HEAD_DIM ** -0.5).astype(BF16)
    k_ref[...] = proj(c_k, c_k + NA_WIDTH).astype(BF16)
    v_ref[...] = proj(c_v, c_v + NA_WIDTH).astype(BF16)


def _in_proj(layer, lgf, lgb, x2, gains, w, cos_t, sin_t, seq):
    n, d = x2.shape
    tm = TM_PROJ
    pos_blocks = seq // tm
    tok = lambda i: (i, 0)
    smem = pl.BlockSpec(memory_space=pltpu.SMEM)
    widths = (NA_WIDTH,) * 3 + (RET_QK_WIDTH,) * 6 + (RET_V_WIDTH,) * 2
    return pl.pallas_call(
        functools.partial(_in_proj_kernel, layer=layer),
        out_shape=[jax.ShapeDtypeStruct((n, w_), BF16) for w_ in widths],
        grid=(n // tm,),
        in_specs=[
            smem, smem,
            pl.BlockSpec((tm, d), tok),
            _layer_block(gains, layer, 1),
            _layer_block(w, layer, 1, pipeline_mode=pl.Buffered(1)),
            pl.BlockSpec((tm, LANES), lambda i: (i % pos_blocks, 0)),
            pl.BlockSpec((tm, LANES), lambda i: (i % pos_blocks, 0)),
        ],
        out_specs=[pl.BlockSpec((tm, w_), tok) for w_ in widths],
        scratch_shapes=[pltpu.VMEM((4, RET_CHUNK, RET_QK_WIDTH), F32)],
        compiler_params=pltpu.CompilerParams(
            dimension_semantics=("arbitrary",), vmem_limit_bytes=VMEM_LIMIT),
        name="in_proj",
    )(lgf, lgb, x2, gains, w, cos_t, sin_t)


def _natten_bias_table(rpb):
    depth = rpb.shape[0]
    qc = np.arange(GRID_W)[:, None]
    kc = np.arange(GRID_W)[None, :]
    dc = np.clip(kc - qc, -(NA_COLS - 1), NA_COLS - 1) + NA_COLS - 1
    lo = np.clip(qc - NA_COLS // 2, 0, GRID_W - NA_COLS)
    valid = (kc >= lo) & (kc < lo + NA_COLS)
    onehot = (dc[None] == np.arange(2 * NA_COLS - 1)[:, None, None]).astype(np.float32)
    t = jnp.einsum('lhdc,cqk->lhdqk', rpb.astype(F32), jnp.asarray(onehot),
                   precision=lax.Precision.HIGHEST)
    t = jnp.where(jnp.asarray(valid), t, NEG)

    def row_pairs(u):
        return jnp.concatenate([u[:, :, :-1], u[:, :, 1:]], axis=-1)

    n_off = 2 * NA_ROWS - 2
    t2 = row_pairs(t).reshape(depth, HEAD_PAIRS_NA, 2, n_off, GRID_W, 2 * GRID_W)
    return jnp.transpose(t2, (0, 1, 3, 2, 4, 5)).reshape(
        depth, HEAD_PAIRS_NA, n_off, 2 * GRID_W, 2 * GRID_W)


def _chunk_kv(k_decayed, v_ref, rows, p):
    v2 = v_ref[0, rows, 2 * p * RET_V_DIM:(2 * p + 2) * RET_V_DIM]
    t = lax.dot_general(k_decayed, v2, _TN, preferred_element_type=F32)
    return jnp.where(_second_head_rows(), t[:, RET_V_DIM:], t[:, :RET_V_DIM])


def _per_head_states(state):
    second = _second_head_rows()
    return (jnp.where(second, 0.0, state).astype(BF16), jnp.where(second, state, 0.0).astype(BF16))


def _ret_bwd_kernel(lgb_ref, kb_ref, v_ref, sb_ref, state_ref, *, layer):
    @pl.when(pl.program_id(1) == 0)
    def _():
        state_ref[...] = jnp.zeros_like(state_ref)

    c = RET_CHUNK
    chunks = range(RET_BWD_CHUNKS_PER_STEP)
    pairs = range(HEAD_PAIRS_RET)
    upd = {}
    for p in pairs:
        for cc in chunks:
            rows = slice(cc * c, (cc + 1) * c)
            upd[cc, p] = _chunk_kv(kb_ref[0, rows, p * LANES:(p + 1) * LANES], v_ref, rows, p)
    for p in pairs:
        decay = jnp.exp(_pair_row_values(lgb_ref, layer, p) * float(c))
        state = state_ref[p]
        for cc in reversed(chunks):
            sb_ref[0, cc, 2 * p], sb_ref[0, cc, 2 * p + 1] = _per_head_states(state)
            state = decay * state + upd[cc, p]
        state_ref[p] = state


def _ret_bwd(layer, lgb, kb, rv):
    b, s, _ = kb.shape
    nch = RET_BWD_CHUNKS_PER_STEP
    tc = nch * RET_CHUNK
    steps = s // tc
    rev = lambda i, j: (i, steps - 1 - j, 0)
    return pl.pallas_call(
        functools.partial(_ret_bwd_kernel, layer=layer),
        out_shape=jax.ShapeDtypeStruct((b, s // RET_CHUNK, RET_HEADS, LANES, RET_V_DIM), BF16),
        grid=(b, steps),
        in_specs=[
            pl.BlockSpec(memory_space=pltpu.SMEM),
            pl.BlockSpec((1, tc, RET_QK_WIDTH), rev),
            pl.BlockSpec((1, tc, RET_V_WIDTH), rev),
        ],
        out_specs=pl.BlockSpec((1, nch, RET_HEADS, LANES, RET_V_DIM),
                               lambda i, j: (i, steps - 1 - j, 0, 0, 0)),
        scratch_shapes=[pltpu.VMEM((HEAD_PAIRS_RET, LANES, RET_V_DIM), F32)],
        compiler_params=pltpu.CompilerParams(
            dimension_semantics=("parallel", "arbitrary"), vmem_limit_bytes=VMEM_LIMIT),
        name="ret_bwd",
    )(lgb, kb, rv)


def _ret_fwd_kernel(lgf_ref, lgb_ref, q_ref, k_ref, qf_ref, qb_ref, kf_ref, v_ref, gate_ref, sb_ref,
                    gain_ref, o_ref, state_ref, dmat_ref, *, layer):
    c = RET_CHUNK

    @pl.when(pl.program_id(1) == 0)
    def _():
        state_ref[...] = jnp.zeros_like(state_ref)
        i = lax.broadcasted_iota(jnp.int32, (c, c), 0)
        j = lax.broadcasted_iota(jnp.int32, (c, c), 1)
        diff = (i - j).astype(F32)
        for h in range(RET_HEADS):
            dmat_ref[h // 2, (h % 2) * c:(h % 2 + 1) * c, :] = jnp.exp(
                jnp.where(diff >= 0, lgf_ref[layer, h] * diff, -lgb_ref[layer, h] * diff))

    second = _lane_is_second_head()
    chunks = range(RET_FWD_CHUNKS_PER_STEP)
    pairs = range(HEAD_PAIRS_RET)
    rows = [slice(cc * c, (cc + 1) * c) for cc in chunks]
    cols = [slice(p * LANES, (p + 1) * LANES) for p in pairs]

    upd = {(cc, p): _chunk_kv(kf_ref[0, rows[cc], cols[p]], v_ref, rows[cc], p)
           for cc in chunks for p in pairs}

    scores = {}
    for cc in chunks:
        for p in pairs:
            qp = q_ref[0, rows[cc], cols[p]]
            q2 = jnp.concatenate([jnp.where(second, jnp.zeros_like(qp), qp),
                                  jnp.where(second, qp, jnp.zeros_like(qp))], axis=0)
            s = lax.dot_general(q2, k_ref[0, rows[cc], cols[p]], _NT, preferred_element_type=F32)
            scores[cc, p] = (s * dmat_ref[p]).astype(BF16)

    before = {}
    for p in pairs:
        decay = jnp.exp(_pair_row_values(lgf_ref, layer, p) * float(c))
        state = state_ref[p]
        for cc in chunks:
            before[cc, 2 * p], before[cc, 2 * p + 1] = _per_head_states(state)
            state = decay * state + upd[cc, p]
        state_ref[p] = state

    for cc in chunks:
        for p in pairs:
            qf = qf_ref[0, rows[cc], cols[p]]
            qb = qb_ref[0, rows[cc], cols[p]]
            for hh in range(2):
                h = 2 * p + hh
                vcols = slice(h * RET_V_DIM, (h + 1) * RET_V_DIM)
                lhs = jnp.concatenate([scores[cc, p][hh * c:(hh + 1) * c], qf, qb], axis=-1)
                rhs = jnp.concatenate([v_ref[0, rows[cc], vcols], before[cc, h], sb_ref[0, cc, h]], axis=0)
                y = jnp.dot(lhs, rhs, preferred_element_type=F32)
                mu = jnp.mean(y, axis=-1, keepdims=True)
                yc = y - mu
                var = jnp.mean(yc * yc, axis=-1, keepdims=True)
                yn = yc * lax.rsqrt(var + EPS) * gain_ref[:, vcols]
                o_ref[0, rows[cc], vcols] = (gate_ref[0, rows[cc], vcols].astype(F32) * yn).astype(BF16)


def _ret_fwd(layer, lgf, lgb, rq, rk, qf, qb, kf, rv, gate, sb, gains):
    b, s, _ = rq.shape
    nch = RET_FWD_CHUNKS_PER_STEP
    tc = nch * RET_CHUNK
    tok = lambda i, j: (i, j, 0)
    smem = pl.BlockSpec(memory_space=pltpu.SMEM)
    qk_spec = pl.BlockSpec((1, tc, RET_QK_WIDTH), tok)
    v_spec = pl.BlockSpec((1, tc, RET_V_WIDTH), tok)
    return pl.pallas_call(
        functools.partial(_ret_fwd_kernel, layer=layer),
        out_shape=jax.ShapeDtypeStruct((b, s, RET_V_WIDTH), BF16),
        grid=(b, s // tc),
        in_specs=[
            smem, smem,
            qk_spec, qk_spec, qk_spec, qk_spec, qk_spec, v_spec, v_spec,
            pl.BlockSpec((1, nch, RET_HEADS, LANES, RET_V_DIM), lambda i, j: (i, j, 0, 0, 0)),
            _layer_block(gains, layer, 2),
        ],
        out_specs=pl.BlockSpec((1, tc, RET_V_WIDTH), tok),
        scratch_shapes=[
            pltpu.VMEM((HEAD_PAIRS_RET, LANES, RET_V_DIM), F32),
            pltpu.VMEM((HEAD_PAIRS_RET, 2 * RET_CHUNK, RET_CHUNK), F32),
        ],
        compiler_params=pltpu.CompilerParams(
            dimension_semantics=("arbitrary", "arbitrary"), vmem_limit_bytes=VMEM_LIMIT),
        name="ret_fwd",
    )(lgf, lgb, rq, rk, qf, qb, kf, rv, gate, sb, gains)


def _natten_mlp_kernel(q_ref, k_ref, v_ref, bias_ref, x_ref, ret_ref, wo_ref, gain_ref, wu_ref, wd_ref,
                       gfin_ref, o_ref, na_ref, *, rows, tiles_per_batch, n_tiles, final_norm):
    step = pl.program_id(0)

    @pl.when(step == 0)
    def _():
        na_ref[...] = jnp.zeros_like(na_ref)

    second = _lane_is_second_head()
    pairs = range(HEAD_PAIRS_NA)
    cols = [slice(p * LANES, (p + 1) * LANES) for p in pairs]
    keys = NA_ROWS * GRID_W
    tile = jnp.minimum(step, n_tiles - 1)
    r0 = (tile % tiles_per_batch) * NA_ROWS_PER_STEP

    def offsets(rr):
        r = r0 + rr
        rs = jnp.clip(r - NA_ROWS // 2, 0, rows - NA_ROWS)
        d0 = rs - r + NA_ROWS - 1
        return rr * GRID_W, pl.multiple_of(rs * GRID_W, GRID_W), d0

    def row_scores(rr):
        qoff, koff, d0 = offsets(rr)
        scores = []
        for p in pairs:
            qp = q_ref[qoff:qoff + GRID_W, cols[p]]
            q2 = jnp.concatenate([jnp.where(second, jnp.zeros_like(qp), qp),
                                  jnp.where(second, qp, jnp.zeros_like(qp))], axis=0)
            kp = k_ref[0, pl.ds(koff, keys), cols[p]]
            bias = jnp.concatenate([bias_ref[p, d0 + 2 * j] for j in range(NA_ROWS // 2)], axis=-1)
            scores.append(lax.dot_general(q2, kp, _NT, preferred_element_type=F32) + bias)
        return scores

    def softmax_terms(scores):
        terms = []
        for s in scores:
            e = jnp.exp(s - jnp.max(s, axis=-1, keepdims=True))
            terms.append((e.astype(BF16), jnp.sum(e, axis=-1, keepdims=True)))
        return terms

    def row_outputs(rr, terms):
        qoff, koff, _ = offsets(rr)
        for p in pairs:
            e, denom = terms[p]
            vp = v_ref[0, pl.ds(koff, keys), cols[p]]
            o2 = jnp.dot(e, vp, preferred_element_type=F32) / denom
            out = jnp.where(second, o2[GRID_W:], o2[:GRID_W])
            na_ref[qoff:qoff + GRID_W, cols[p]] = out.astype(BF16)

    mix = (jnp.dot(na_ref[...], wo_ref[:NA_WIDTH, :], preferred_element_type=F32)
           + jnp.dot(ret_ref[...], wo_ref[NA_WIDTH:, :], preferred_element_type=F32))
    x1 = x_ref[...] + mix
    h = _rms(x1, gain_ref[...]).astype(BF16)

    d_ff = wu_ref.shape[1]
    n_chunks = d_ff // FF_CHUNK
    rows_per_chunk = NA_ROWS_PER_STEP // n_chunks
    chunk_rows = lambda c: range(c * rows_per_chunk, (c + 1) * rows_per_chunk)
    scores = {rr: row_scores(rr) for rr in chunk_rows(0)}
    mlp = None
    for c in range(n_chunks):
        f0 = c * FF_CHUNK
        u = jnp.maximum(jnp.dot(h, wu_ref[:, f0:f0 + FF_CHUNK], preferred_element_type=F32), 0.0)
        for rr in chunk_rows(c):
            row_outputs(rr, softmax_terms(scores.pop(rr)))
        if c + 1 < n_chunks:
            for rr in chunk_rows(c + 1):
                scores[rr] = row_scores(rr)
        t = jnp.dot((u * u).astype(BF16), wd_ref[f0:f0 + FF_CHUNK, :], preferred_element_type=F32)
        mlp = t if mlp is None else mlp + t
    x3 = x1 + mlp
    o_ref[...] = _rms(x3, gfin_ref[...]) if final_norm else x3


def _natten_mlp(layer, q, k, v, bias, x2, ret, wo, gains, wu, wd, gfin, final_norm):
    n, d = x2.shape
    b, s, w = k.shape
    rows = s // GRID_W
    tm = NA_ROWS_PER_STEP * GRID_W
    n_tiles = n // tm
    tiles_per_batch = s // tm
    att = lambda i: jnp.minimum(i, n_tiles - 1)
    prev = lambda i: jnp.maximum(i - 1, 0)
    resident = lambda a: _layer_block(a, layer, 1, pipeline_mode=pl.Buffered(1))
    whole_batch = pl.BlockSpec((1, s, w), lambda i: (att(i) // tiles_per_batch, 0, 0),
                               pipeline_mode=pl.Buffered(1))
    return pl.pallas_call(
        functools.partial(_natten_mlp_kernel, rows=rows, tiles_per_batch=tiles_per_batch,
                          n_tiles=n_tiles, final_norm=final_norm),
        out_shape=jax.ShapeDtypeStruct((n, d), F32),
        grid=(n_tiles + 1,),
        in_specs=[
            pl.BlockSpec((tm, w), lambda i: (att(i), 0)),
            whole_batch,
            whole_batch,
            resident(bias),
            pl.BlockSpec((tm, d), lambda i: (prev(i), 0)),
            pl.BlockSpec((tm, RET_V_WIDTH), lambda i: (prev(i), 0)),
            resident(wo),
            _layer_block(gains, layer, 1),
            resident(wu),
            resident(wd),
            pl.BlockSpec((1, d), lambda i: (0, 0)),
        ],
        out_specs=pl.BlockSpec((tm, d), lambda i: (prev(i), 0)),
        scratch_shapes=[pltpu.VMEM((tm, NA_WIDTH), BF16)],
        compiler_params=pltpu.CompilerParams(
            dimension_semantics=("arbitrary",), vmem_limit_bytes=VMEM_LIMIT),
        name="natten_mlp",
    )(q, k, v, bias, x2, ret, wo, gains, wu, wd, gfin)


def _rotary_tables(seq):
    half = RET_QK_DIM // 2
    inv = 1.0 / (ROPE_BASE ** (jnp.arange(0, RET_QK_DIM, 2, dtype=F32) / RET_QK_DIM))
    ang = jnp.arange(seq).astype(F32)[:, None] * inv[None, :]
    cos = jnp.cos(ang)
    sin = jnp.sin(ang)
    cos_t = jnp.tile(cos, (1, LANES // half))
    sin_t = jnp.tile(jnp.concatenate([-sin, sin], axis=-1), (1, LANES // RET_QK_DIM))
    return cos_t, sin_t


def kernel(x, w_in, w_out, na_rpb, ret_decay_fwd, ret_decay_bwd, ret_norm_gain,
           norm_mix, norm_mlp, w_up, w_down, norm_final):
    bsz, seq, d = x.shape
    depth = w_in.shape[0]
    n = bsz * seq
    rows = seq // GRID_W
    assert TM_PROJ % RET_CHUNK == 0 and seq % TM_PROJ == 0
    assert seq % (RET_BWD_CHUNKS_PER_STEP * RET_CHUNK) == 0
    assert seq % (RET_FWD_CHUNKS_PER_STEP * RET_CHUNK) == 0
    assert rows % NA_ROWS_PER_STEP == 0 and rows >= NA_ROWS
    assert w_up.shape[-1] % FF_CHUNK == 0 and NA_ROWS_PER_STEP % (w_up.shape[-1] // FF_CHUNK) == 0

    cos_t, sin_t = _rotary_tables(seq)
    w_in_b, w_out_b, w_up_b, w_down_b = (w.astype(BF16) for w in (w_in, w_out, w_up, w_down))
    lgf = jnp.log1p(-jnp.exp(ret_decay_fwd.astype(F32)))
    lgb = jnp.log1p(-jnp.exp(ret_decay_bwd.astype(F32)))
    bias = _natten_bias_table(na_rpb)
    g_mix = norm_mix.astype(F32).reshape(depth, 1, d)
    g_mlp = norm_mlp.astype(F32).reshape(depth, 1, d)
    g_ret = ret_norm_gain.astype(F32).reshape(depth, 1, RET_V_WIDTH)
    g_fin = norm_final.astype(F32).reshape(1, d)

    x2 = x.reshape(n, d)
    as_seq = lambda t: t.reshape(bsz, seq, t.shape[-1])
    for l in range(depth):
        q, k, v, rq, rk, qf, qb, kf, kb, rv, gate = _in_proj(
            l, lgf, lgb, x2, g_mix, w_in_b, cos_t, sin_t, seq)
        sb = _ret_bwd(l, lgb, as_seq(kb), as_seq(rv))
        ret = _ret_fwd(l, lgf, lgb, as_seq(rq), as_seq(rk), as_seq(qf), as_seq(qb), as_seq(kf),
                       as_seq(rv), as_seq(gate), sb, g_ret)
        x2 = _natten_mlp(l, q, as_seq(k), as_seq(v), bias, x2, ret.reshape(n, RET_V_WIDTH),
                         w_out_b, g_mlp, w_up_b, w_down_b, g_fin, l == depth - 1)
    return x2.reshape(bsz, seq, d)
```

```python
import functools

import jax
import jax.numpy as jnp
import numpy as np
from jax import lax
from jax.experimental import pallas as pl
from jax.experimental.pallas import tpu as pltpu

F32 = jnp.float32
BF16 = jnp.bfloat16

GRID_W = 64
NA_HEADS = 8
NA_HEAD_DIM = 64
NA_WIDTH = NA_HEADS * NA_HEAD_DIM
NA_ROWS = 8
NA_COLS = 16
RET_HEADS = 4
RET_QK_DIM = 64
RET_V_DIM = 128
RET_QK_WIDTH = RET_HEADS * RET_QK_DIM
RET_V_WIDTH = RET_HEADS * RET_V_DIM
RET_CHUNK = 128
ROPE_BASE = 10000.0
EPS = 1e-6
NEG = -1e30

LANES = 128
HEAD_PAIRS_NA = NA_WIDTH // LANES
HEAD_PAIRS_RET = RET_QK_WIDTH // LANES
VMEM_LIMIT = 56 * 1024 * 1024

TM_PROJ = 1024
FF_CHUNK = 1024
NA_ROWS_PER_STEP = 8
RET_BWD_CHUNKS_PER_STEP = 16
RET_FWD_CHUNKS_PER_STEP = 8

_NT = (((1,), (1,)), ((), ()))
_TN = (((0,), (0,)), ((), ()))


def _rms(x, gain):
    return x * lax.rsqrt(jnp.mean(x * x, axis=-1, keepdims=True) + EPS) * gain


def _lane_is_second_head():
    return lax.broadcasted_iota(jnp.int32, (1, LANES), 1) >= NA_HEAD_DIM


def _pair_lane_values(smem_ref, layer, p):
    return jnp.where(_lane_is_second_head(), smem_ref[layer, 2 * p + 1], smem_ref[layer, 2 * p])


def _second_head_rows():
    return lax.broadcasted_iota(jnp.int32, (LANES, 1), 0) >= RET_QK_DIM


def _pair_row_values(smem_ref, layer, p):
    return jnp.where(_second_head_rows(), smem_ref[layer, 2 * p + 1], smem_ref[layer, 2 * p])


def _layer_block(a, layer, grid_rank, **kw):
    zeros = (0,) * (a.ndim - 1)
    if grid_rank == 1:
        index_map = lambda i: (layer,) + zeros
    else:
        index_map = lambda i, j: (layer,) + zeros
    return pl.BlockSpec((None,) + a.shape[1:], index_map, **kw)


def _in_proj_kernel(lgf_ref, lgb_ref, x_ref, gain_ref, w_ref, cos_ref, sin_ref,
                    q_ref, k_ref, v_ref, rq_ref, rk_ref, qf_ref, qb_ref, kf_ref, kb_ref, rv_ref, rg_ref,
                    decay_ref, *, layer):
    c = RET_CHUNK

    @pl.when(pl.program_id(0) == 0)
    def _():
        pos = lax.broadcasted_iota(jnp.int32, (c, 1), 0).astype(F32)
        for p in range(HEAD_PAIRS_RET):
            lanes = slice(p * LANES, (p + 1) * LANES)
            lf = _pair_lane_values(lgf_ref, layer, p)
            lb = _pair_lane_values(lgb_ref, layer, p)
            decay_ref[0, :, lanes] = jnp.exp(lf * (pos + 1.0))
            decay_ref[1, :, lanes] = jnp.exp(lb * (float(c) - pos))
            decay_ref[2, :, lanes] = jnp.exp(lf * (float(c) - 1.0 - pos))
            decay_ref[3, :, lanes] = jnp.exp(lb * pos)

    h = _rms(x_ref[...], gain_ref[...]).astype(BF16)
    tm = h.shape[0]

    def decayed(t, which):
        t3 = t.reshape(tm // c, c, RET_QK_WIDTH) * decay_ref[which][None]
        return t3.reshape(tm, RET_QK_WIDTH).astype(BF16)

    def proj(c0, c1):
        return jnp.dot(h, w_ref[:, c0:c1], preferred_element_type=F32)

    starts = np.cumsum([0, NA_WIDTH, NA_WIDTH, NA_WIDTH, RET_QK_WIDTH, RET_QK_WIDTH, RET_V_WIDTH])
    c_q, c_k, c_v, c_rq, c_rk, c_rv, c_rg = (int(s) for s in starts)

    cos = cos_ref[...]
    sin = sin_ref[...]
    lane = lax.broadcasted_iota(jnp.int32, (1, LANES), 1)
    first_half = (lane % RET_QK_DIM) < RET_QK_DIM // 2

    def rotary(t, scale):
        parts = []
        for p in range(HEAD_PAIRS_RET):
            tp = t[:, p * LANES:(p + 1) * LANES]
            swapped = jnp.where(first_half,
                                pltpu.roll(tp, LANES - RET_QK_DIM // 2, 1),
                                pltpu.roll(tp, RET_QK_DIM // 2, 1))
            parts.append((tp * cos + swapped * sin) * scale)
        return jnp.concatenate(parts, axis=-1)

    rqk = proj(c_rq, c_rk + RET_QK_WIDTH)
    rq = rotary(rqk[:, :RET_QK_WIDTH], 1.0)
    rq_ref[...] = rq.astype(BF16)
    qf_ref[...] = decayed(rq, 0)
    qb_ref[...] = decayed(rq, 1)
    rk = rotary(rqk[:, RET_QK_WIDTH:], RET_QK_DIM ** -0.5)
    rk_ref[...] = rk.astype(BF16)
    kf_ref[...] = decayed(rk, 2)
    kb_ref[...] = decayed(rk, 3)
    g = proj(c_rg, c_rg + RET_V_WIDTH)
    rg_ref[...] = (g / (1.0 + jnp.exp(-g))).astype(BF16)
    rv_ref[...] = proj(c_rv, c_rv + RET_V_WIDTH).astype(BF16)
    q_ref[...] = (proj(c_q, c_q + NA_WIDTH) * NA_HEAD_DIM ** -0.5).astype(BF16)
    k_ref[...] = proj(c_k, c_k + NA_WIDTH).astype(BF16)
    v_ref[...] = proj(c_v, c_v + NA_WIDTH).astype(BF16)


def _in_proj(layer, lgf, lgb, x2, gains, w, cos_t, sin_t, seq):
    n, d = x2.shape
    tm = TM_PROJ
    pos_blocks = seq // tm
    tok = lambda i: (i, 0)
    smem = pl.BlockSpec(memory_space=pltpu.SMEM)
    widths = (NA_WIDTH,) * 3 + (RET_QK_WIDTH,) * 6 + (RET_V_WIDTH,) * 2
    return pl.pallas_call(
        functools.partial(_in_proj_kernel, layer=layer),
        out_shape=[jax.ShapeDtypeStruct((n, w_), BF16) for w_ in widths],
        grid=(n // tm,),
        in_specs=[
            smem, smem,
            pl.BlockSpec((tm, d), tok),
            _layer_block(gains, layer, 1),
            _layer_block(w, layer, 1, pipeline_mode=pl.Buffered(1)),
            pl.BlockSpec((tm, LANES), lambda i: (i % pos_blocks, 0)),
            pl.BlockSpec((tm, LANES), lambda i: (i % pos_blocks, 0)),
        ],
        out_specs=[pl.BlockSpec((tm, w_), tok) for w_ in widths],
        scratch_shapes=[pltpu.VMEM((4, RET_CHUNK, RET_QK_WIDTH), F32)],
        compiler_params=pltpu.CompilerParams(
            dimension_semantics=("arbitrary",), vmem_limit_bytes=VMEM_LIMIT),
        name="in_proj",
    )(lgf, lgb, x2, gains, w, cos_t, sin_t)


def _natten_bias_table(rpb):
    depth = rpb.shape[0]
    qc = np.arange(GRID_W)[:, None]
    kc = np.arange(GRID_W)[None, :]
    dc = np.clip(kc - qc, -(NA_COLS - 1), NA_COLS - 1) + NA_COLS - 1
    lo = np.clip(qc - NA_COLS // 2, 0, GRID_W - NA_COLS)
    valid = (kc >= lo) & (kc < lo + NA_COLS)
    n_off = 2 * NA_ROWS - 2
    r = rpb.astype(F32)
    r = jnp.stack([r[:, :, :-1], r[:, :, 1:]], axis=3).reshape(
        depth, HEAD_PAIRS_NA, 2, n_off, 2, 2 * NA_COLS - 1)
    onehot = (dc[None] == np.arange(2 * NA_COLS - 1)[:, None, None]).astype(np.float32)
    t = jnp.einsum('lphdwc,cqk->lpdhqwk', r, jnp.asarray(onehot), precision=lax.Precision.HIGHEST)
    t = jnp.where(jnp.asarray(valid)[:, None, :], t, NEG)
    return t.reshape(depth, HEAD_PAIRS_NA, n_off, 2 * GRID_W, 2 * GRID_W)


def _chunk_kv(k_decayed, v_ref, rows, p):
    v2 = v_ref[0, rows, 2 * p * RET_V_DIM:(2 * p + 2) * RET_V_DIM]
    t = lax.dot_general(k_decayed, v2, _TN, preferred_element_type=F32)
    return jnp.where(_second_head_rows(), t[:, RET_V_DIM:], t[:, :RET_V_DIM])


def _per_head_states(state):
    second = _second_head_rows()
    return (jnp.where(second, 0.0, state).astype(BF16), jnp.where(second, state, 0.0).astype(BF16))


def _ret_bwd_kernel(lgb_ref, kb_ref, v_ref, sb_ref, state_ref, *, layer):
    @pl.when(pl.program_id(1) == 0)
    def _():
        state_ref[...] = jnp.zeros_like(state_ref)

    c = RET_CHUNK
    chunks = range(RET_BWD_CHUNKS_PER_STEP)
    pairs = range(HEAD_PAIRS_RET)
    upd = {}
    for p in pairs:
        for cc in chunks:
            rows = slice(cc * c, (cc + 1) * c)
            upd[cc, p] = _chunk_kv(kb_ref[0, rows, p * LANES:(p + 1) * LANES], v_ref, rows, p)
    for p in pairs:
        decay = jnp.exp(_pair_row_values(lgb_ref, layer, p) * float(c))
        state = state_ref[p]
        for cc in reversed(chunks):
            sb_ref[0, cc, 2 * p], sb_ref[0, cc, 2 * p + 1] = _per_head_states(state)
            state = decay * state + upd[cc, p]
        state_ref[p] = state


def _ret_bwd(layer, lgb, kb, rv):
    b, s, _ = kb.shape
    nch = RET_BWD_CHUNKS_PER_STEP
    tc = nch * RET_CHUNK
    steps = s // tc
    rev = lambda i, j: (i, steps - 1 - j, 0)
    return pl.pallas_call(
        functools.partial(_ret_bwd_kernel, layer=layer),
        out_shape=jax.ShapeDtypeStruct((b, s // RET_CHUNK, RET_HEADS, LANES, RET_V_DIM), BF16),
        grid=(b, steps),
        in_specs=[
            pl.BlockSpec(memory_space=pltpu.SMEM),
            pl.BlockSpec((1, tc, RET_QK_WIDTH), rev),
            pl.BlockSpec((1, tc, RET_V_WIDTH), rev),
        ],
        out_specs=pl.BlockSpec((1, nch, RET_HEADS, LANES, RET_V_DIM),
                               lambda i, j: (i, steps - 1 - j, 0, 0, 0)),
        scratch_shapes=[pltpu.VMEM((HEAD_PAIRS_RET, LANES, RET_V_DIM), F32)],
        compiler_params=pltpu.CompilerParams(
            dimension_semantics=("parallel", "arbitrary"), vmem_limit_bytes=VMEM_LIMIT),
        name="ret_bwd",
    )(lgb, kb, rv)


def _ret_fwd_kernel(lgf_ref, lgb_ref, q_ref, k_ref, qf_ref, qb_ref, kf_ref, v_ref, gate_ref, sb_ref,
                    gain_ref, o_ref, state_ref, dmat_ref, *, layer):
    c = RET_CHUNK

    @pl.when(pl.program_id(1) == 0)
    def _():
        state_ref[...] = jnp.zeros_like(state_ref)
        i = lax.broadcasted_iota(jnp.int32, (c, c), 0)
        j = lax.broadcasted_iota(jnp.int32, (c, c), 1)
        diff = (i - j).astype(F32)
        for h in range(RET_HEADS):
            dmat_ref[h // 2, (h % 2) * c:(h % 2 + 1) * c, :] = jnp.exp(
                jnp.where(diff >= 0, lgf_ref[layer, h] * diff, -lgb_ref[layer, h] * diff))

    second = _lane_is_second_head()
    chunks = range(RET_FWD_CHUNKS_PER_STEP)
    pairs = range(HEAD_PAIRS_RET)
    rows = [slice(cc * c, (cc + 1) * c) for cc in chunks]
    cols = [slice(p * LANES, (p + 1) * LANES) for p in pairs]

    upd = {(cc, p): _chunk_kv(kf_ref[0, rows[cc], cols[p]], v_ref, rows[cc], p)
           for cc in chunks for p in pairs}

    scores = {}
    for cc in chunks:
        for p in pairs:
            qp = q_ref[0, rows[cc], cols[p]]
            q2 = jnp.concatenate([jnp.where(second, jnp.zeros_like(qp), qp),
                                  jnp.where(second, qp, jnp.zeros_like(qp))], axis=0)
            s = lax.dot_general(q2, k_ref[0, rows[cc], cols[p]], _NT, preferred_element_type=F32)
            scores[cc, p] = (s * dmat_ref[p]).astype(BF16)

    before = {}
    for p in pairs:
        decay = jnp.exp(_pair_row_values(lgf_ref, layer, p) * float(c))
        state = state_ref[p]
        for cc in chunks:
            before[cc, 2 * p], before[cc, 2 * p + 1] = _per_head_states(state)
            state = decay * state + upd[cc, p]
        state_ref[p] = state

    for cc in chunks:
        for p in pairs:
            qf = qf_ref[0, rows[cc], cols[p]]
            qb = qb_ref[0, rows[cc], cols[p]]
            for hh in range(2):
                h = 2 * p + hh
                vcols = slice(h * RET_V_DIM, (h + 1) * RET_V_DIM)
                lhs = jnp.concatenate([scores[cc, p][hh * c:(hh + 1) * c], qf, qb], axis=-1)
                rhs = jnp.concatenate([v_ref[0, rows[cc], vcols], before[cc, h], sb_ref[0, cc, h]], axis=0)
                y = jnp.dot(lhs, rhs, preferred_element_type=F32)
                mu = jnp.mean(y, axis=-1, keepdims=True)
                yc = y - mu
                var = jnp.mean(yc * yc, axis=-1, keepdims=True)
                yn = yc * lax.rsqrt(var + EPS) * gain_ref[:, vcols]
                o_ref[0, rows[cc], vcols] = (gate_ref[0, rows[cc], vcols].astype(F32) * yn).astype(BF16)


def _ret_fwd(layer, lgf, lgb, rq, rk, qf, qb, kf, rv, gate, sb, gains):
    b, s, _ = rq.shape
    nch = RET_FWD_CHUNKS_PER_STEP
    tc = nch * RET_CHUNK
    tok = lambda i, j: (i, j, 0)
    smem = pl.BlockSpec(memory_space=pltpu.SMEM)
    qk_spec = pl.BlockSpec((1, tc, RET_QK_WIDTH), tok)
    v_spec = pl.BlockSpec((1, tc, RET_V_WIDTH), tok)
    return pl.pallas_call(
        functools.partial(_ret_fwd_kernel, layer=layer),
        out_shape=jax.ShapeDtypeStruct((b, s, RET_V_WIDTH), BF16),
        grid=(b, s // tc),
        in_specs=[
            smem, smem,
            qk_spec, qk_spec, qk_spec, qk_spec, qk_spec, v_spec, v_spec,
            pl.BlockSpec((1, nch, RET_HEADS, LANES, RET_V_DIM), lambda i, j: (i, j, 0, 0, 0)),
            _layer_block(gains, layer, 2),
        ],
        out_specs=pl.BlockSpec((1, tc, RET_V_WIDTH), tok),
        scratch_shapes=[
            pltpu.VMEM((HEAD_PAIRS_RET, LANES, RET_V_DIM), F32),
            pltpu.VMEM((HEAD_PAIRS_RET, 2 * RET_CHUNK, RET_CHUNK), F32),
        ],
        compiler_params=pltpu.CompilerParams(
            dimension_semantics=("arbitrary", "arbitrary"), vmem_limit_bytes=VMEM_LIMIT),
        name="ret_fwd",
    )(lgf, lgb, rq, rk, qf, qb, kf, rv, gate, sb, gains)


def _natten_mlp_kernel(q_ref, k_prev_ref, k_cur_ref, k_next_ref, v_prev_ref, v_cur_ref, v_next_ref, bias_ref,
                       x_ref, ret_ref, wo_ref, gain_ref, wu_ref, wd_ref, gfin_ref,
                       o_ref, na_ref, kwin_ref, vwin_ref, *, rows, tiles_per_batch, n_tiles, final_norm):
    step = pl.program_id(0)
    tm = NA_ROWS_PER_STEP * GRID_W

    def load_windows():
        for win_ref, parts in ((kwin_ref, (k_prev_ref, k_cur_ref, k_next_ref)),
                               (vwin_ref, (v_prev_ref, v_cur_ref, v_next_ref))):
            for i, part in enumerate(parts):
                win_ref[i * tm:(i + 1) * tm, :] = part[0]

    second = _lane_is_second_head()
    pairs = range(HEAD_PAIRS_NA)
    cols = [slice(p * LANES, (p + 1) * LANES) for p in pairs]
    keys = NA_ROWS * GRID_W
    tile = jnp.minimum(step, n_tiles - 1)
    r0 = (tile % tiles_per_batch) * NA_ROWS_PER_STEP

    def offsets(rr):
        r = r0 + rr
        rs = jnp.clip(r - NA_ROWS // 2, 0, rows - NA_ROWS)
        d0 = rs - r + NA_ROWS - 1
        return rr * GRID_W, pl.multiple_of((rs - r0 + NA_ROWS_PER_STEP) * GRID_W, GRID_W), d0

    def row_scores(rr):
        qoff, koff, d0 = offsets(rr)
        scores = []
        for p in pairs:
            qp = q_ref[qoff:qoff + GRID_W, cols[p]]
            q2 = jnp.concatenate([jnp.where(second, jnp.zeros_like(qp), qp),
                                  jnp.where(second, qp, jnp.zeros_like(qp))], axis=0)
            kp = kwin_ref[pl.ds(koff, keys), cols[p]]
            bias = jnp.concatenate([bias_ref[p, d0 + 2 * j] for j in range(NA_ROWS // 2)], axis=-1)
            scores.append(lax.dot_general(q2, kp, _NT, preferred_element_type=F32) + bias)
        return scores

    def softmax_terms(scores):
        terms = []
        for s in scores:
            e = jnp.exp(s - jnp.max(s, axis=-1, keepdims=True))
            terms.append((e.astype(BF16), jnp.sum(e, axis=-1, keepdims=True)))
        return terms

    def row_outputs(rr, terms):
        qoff, koff, _ = offsets(rr)
        for p in pairs:
            e, denom = terms[p]
            vp = vwin_ref[pl.ds(koff, keys), cols[p]]
            o2 = jnp.dot(e, vp, preferred_element_type=F32) / denom
            out = jnp.where(second, o2[GRID_W:], o2[:GRID_W])
            na_ref[qoff:qoff + GRID_W, cols[p]] = out.astype(BF16)

    d_ff = wu_ref.shape[1]
    n_chunks = d_ff // FF_CHUNK
    rows_per_chunk = NA_ROWS_PER_STEP // n_chunks
    chunk_rows = lambda c: range(c * rows_per_chunk, (c + 1) * rows_per_chunk)

    def body(attend, dense):
        if dense:
            mix = (jnp.dot(na_ref[...], wo_ref[:NA_WIDTH, :], preferred_element_type=F32)
                   + jnp.dot(ret_ref[...], wo_ref[NA_WIDTH:, :], preferred_element_type=F32))
            x1 = x_ref[...] + mix
            h = _rms(x1, gain_ref[...]).astype(BF16)
        if attend:
            load_windows()
            scores = {rr: row_scores(rr) for rr in chunk_rows(0)}
        mlp = None
        for c in range(n_chunks):
            f0 = c * FF_CHUNK
            if dense:
                u = jnp.maximum(jnp.dot(h, wu_ref[:, f0:f0 + FF_CHUNK], preferred_element_type=F32), 0.0)
            if attend:
                for rr in chunk_rows(c):
                    row_outputs(rr, softmax_terms(scores.pop(rr)))
                if c + 1 < n_chunks:
                    for rr in chunk_rows(c + 1):
                        scores[rr] = row_scores(rr)
            if dense:
                t = jnp.dot((u * u).astype(BF16), wd_ref[f0:f0 + FF_CHUNK, :], preferred_element_type=F32)
                mlp = t if mlp is None else mlp + t
        if dense:
            x3 = x1 + mlp
            o_ref[...] = _rms(x3, gfin_ref[...]) if final_norm else x3

    pl.when(step == 0)(functools.partial(body, True, False))
    pl.when(jnp.logical_and(step > 0, step < n_tiles))(functools.partial(body, True, True))
    pl.when(step == n_tiles)(functools.partial(body, False, True))


def _natten_mlp(layer, q, k, v, bias, x2, ret, wo, gains, wu, wd, gfin, final_norm):
    n, d = x2.shape
    b, s, w = k.shape
    rows = s // GRID_W
    tm = NA_ROWS_PER_STEP * GRID_W
    n_tiles = n // tm
    tiles_per_batch = s // tm
    att = lambda i: jnp.minimum(i, n_tiles - 1)
    prev = lambda i: jnp.maximum(i - 1, 0)
    resident = lambda a: _layer_block(a, layer, 1, pipeline_mode=pl.Buffered(1))

    def neighbour_tile(shift):
        def index_map(i):
            t = att(i)
            tb = jnp.clip(t % tiles_per_batch + shift, 0, tiles_per_batch - 1)
            return (t // tiles_per_batch, tb, 0)
        return pl.BlockSpec((1, tm, w), index_map)

    halo = [neighbour_tile(-1), neighbour_tile(0), neighbour_tile(1)]
    return pl.pallas_call(
        functools.partial(_natten_mlp_kernel, rows=rows, tiles_per_batch=tiles_per_batch,
                          n_tiles=n_tiles, final_norm=final_norm),
        out_shape=jax.ShapeDtypeStruct((n, d), F32),
        grid=(n_tiles + 1,),
        in_specs=[
            pl.BlockSpec((tm, w), lambda i: (att(i), 0)),
            *halo,
            *halo,
            resident(bias),
            pl.BlockSpec((tm, d), lambda i: (prev(i), 0)),
            pl.BlockSpec((tm, RET_V_WIDTH), lambda i: (prev(i), 0)),
            resident(wo),
            _layer_block(gains, layer, 1),
            resident(wu),
            resident(wd),
            pl.BlockSpec((1, d), lambda i: (0, 0)),
        ],
        out_specs=pl.BlockSpec((tm, d), lambda i: (prev(i), 0)),
        scratch_shapes=[pltpu.VMEM((tm, NA_WIDTH), BF16),
                        pltpu.VMEM((3 * tm, NA_WIDTH), BF16),
                        pltpu.VMEM((3 * tm, NA_WIDTH), BF16)],
        compiler_params=pltpu.CompilerParams(
            dimension_semantics=("arbitrary",), vmem_limit_bytes=VMEM_LIMIT),
        name="natten_mlp",
    )(q, k, k, k, v, v, v, bias, x2, ret, wo, gains, wu, wd, gfin)


def _rotary_tables(seq):
    half = RET_QK_DIM // 2
    inv = 1.0 / (ROPE_BASE ** (jnp.arange(0, RET_QK_DIM, 2, dtype=F32) / RET_QK_DIM))
    ang = jnp.arange(seq).astype(F32)[:, None] * inv[None, :]
    cos = jnp.cos(ang)
    sin = jnp.sin(ang)
    cos_t = jnp.tile(cos, (1, LANES // half))
    sin_t = jnp.tile(jnp.concatenate([-sin, sin], axis=-1), (1, LANES // RET_QK_DIM))
    return cos_t, sin_t


def kernel(x, w_in, w_out, na_rpb, ret_decay_fwd, ret_decay_bwd, ret_norm_gain,
           norm_mix, norm_mlp, w_up, w_down, norm_final):
    bsz, seq, d = x.shape
    depth = w_in.shape[0]
    n = bsz * seq
    rows = seq // GRID_W
    assert TM_PROJ % RET_CHUNK == 0 and seq % TM_PROJ == 0
    assert seq % (RET_BWD_CHUNKS_PER_STEP * RET_CHUNK) == 0
    assert seq % (RET_FWD_CHUNKS_PER_STEP * RET_CHUNK) == 0
    assert rows % NA_ROWS_PER_STEP == 0 and rows >= NA_ROWS and NA_ROWS_PER_STEP >= NA_ROWS // 2
    assert w_up.shape[-1] % FF_CHUNK == 0 and NA_ROWS_PER_STEP % (w_up.shape[-1] // FF_CHUNK) == 0

    cos_t, sin_t = _rotary_tables(seq)
    w_in_b, w_out_b, w_up_b, w_down_b = (w.astype(BF16) for w in (w_in, w_out, w_up, w_down))
    lgf = jnp.log1p(-jnp.exp(ret_decay_fwd.astype(F32)))
    lgb = jnp.log1p(-jnp.exp(ret_decay_bwd.astype(F32)))
    bias = _natten_bias_table(na_rpb)
    g_mix = norm_mix.astype(F32).reshape(depth, 1, d)
    g_mlp = norm_mlp.astype(F32).reshape(depth, 1, d)
    g_ret = ret_norm_gain.astype(F32).reshape(depth, 1, RET_V_WIDTH)
    g_fin = norm_final.astype(F32).reshape(1, d)

    x2 = x.reshape(n, d)
    as_seq = lambda t: t.reshape(bsz, seq, t.shape[-1])
    for l in range(depth):
        q, k, v, rq, rk, qf, qb, kf, kb, rv, gate = _in_proj(
            l, lgf, lgb, x2, g_mix, w_in_b, cos_t, sin_t, seq)
        sb = _ret_bwd(l, lgb, as_seq(kb), as_seq(rv))
        ret = _ret_fwd(l, lgf, lgb, as_seq(rq), as_seq(rk), as_seq(qf), as_seq(qb), as_seq(kf),
                       as_seq(rv), as_seq(gate), sb, g_ret)
        x2 = _natten_mlp(l, q, as_seq(k), as_seq(v), bias, x2, ret.reshape(n, RET_V_WIDTH),
                         w_out_b, g_mlp, w_up_b, w_down_b, g_fin, l == depth - 1)
    return x2.reshape(bsz, seq, d)
```

```python
import functools

import jax
import jax.numpy as jnp
import numpy as np
from jax import lax
from jax.experimental import pallas as pl
from jax.experimental.pallas import tpu as pltpu

F32 = jnp.float32
BF16 = jnp.bfloat16

GRID_W = 64
NA_HEADS = 8
NA_HEAD_DIM = 64
NA_WIDTH = NA_HEADS * NA_HEAD_DIM
NA_ROWS = 8
NA_COLS = 16
RET_HEADS = 4
RET_QK_DIM = 64
RET_V_DIM = 128
RET_QK_WIDTH = RET_HEADS * RET_QK_DIM
RET_V_WIDTH = RET_HEADS * RET_V_DIM
RET_CHUNK = 128
ROPE_BASE = 10000.0
EPS = 1e-6
NEG = -1e30

LANES = 128
HEAD_PAIRS_NA = NA_WIDTH // LANES
HEAD_PAIRS_RET = RET_QK_WIDTH // LANES
VMEM_LIMIT = 56 * 1024 * 1024

TM_PROJ = 1024
FF_CHUNK = 1024
NA_ROWS_PER_STEP = 8
RET_BWD_CHUNKS_PER_STEP = 16
RET_FWD_CHUNKS_PER_STEP = 16

_NT = (((1,), (1,)), ((), ()))
_TN = (((0,), (0,)), ((), ()))


def _rms(x, gain):
    return x * lax.rsqrt(jnp.mean(x * x, axis=-1, keepdims=True) + EPS) * gain


def _lane_is_second_head():
    return lax.broadcasted_iota(jnp.int32, (1, LANES), 1) >= NA_HEAD_DIM


def _pair_lane_values(smem_ref, layer, p):
    return jnp.where(_lane_is_second_head(), smem_ref[layer, 2 * p + 1], smem_ref[layer, 2 * p])


def _second_head_rows():
    return lax.broadcasted_iota(jnp.int32, (LANES, 1), 0) >= RET_QK_DIM


def _pair_row_values(smem_ref, layer, p):
    return jnp.where(_second_head_rows(), smem_ref[layer, 2 * p + 1], smem_ref[layer, 2 * p])


def _layer_block(a, layer, grid_rank, **kw):
    zeros = (0,) * (a.ndim - 1)
    if grid_rank == 1:
        index_map = lambda i: (layer,) + zeros
    else:
        index_map = lambda i, j: (layer,) + zeros
    return pl.BlockSpec((None,) + a.shape[1:], index_map, **kw)


def _in_proj_kernel(lgf_ref, lgb_ref, x_ref, gain_ref, w_ref, cos_ref, sin_ref,
                    q_ref, k_ref, v_ref, rq_ref, rk_ref, qf_ref, qb_ref, kf_ref, kb_ref, rv_ref, rg_ref,
                    decay_ref, *, layer):
    c = RET_CHUNK

    @pl.when(pl.program_id(0) == 0)
    def _():
        pos = lax.broadcasted_iota(jnp.int32, (c, 1), 0).astype(F32)
        for p in range(HEAD_PAIRS_RET):
            lanes = slice(p * LANES, (p + 1) * LANES)
            lf = _pair_lane_values(lgf_ref, layer, p)
            lb = _pair_lane_values(lgb_ref, layer, p)
            decay_ref[0, :, lanes] = jnp.exp(lf * (pos + 1.0))
            decay_ref[1, :, lanes] = jnp.exp(lb * (float(c) - pos))
            decay_ref[2, :, lanes] = jnp.exp(lf * (float(c) - 1.0 - pos))
            decay_ref[3, :, lanes] = jnp.exp(lb * pos)

    h = _rms(x_ref[...], gain_ref[...]).astype(BF16)
    tm = h.shape[0]

    def decayed(t, which):
        t3 = t.reshape(tm // c, c, RET_QK_WIDTH) * decay_ref[which][None]
        return t3.reshape(tm, RET_QK_WIDTH).astype(BF16)

    def proj(c0, c1):
        return jnp.dot(h, w_ref[:, c0:c1], preferred_element_type=F32)

    starts = np.cumsum([0, NA_WIDTH, NA_WIDTH, NA_WIDTH, RET_QK_WIDTH, RET_QK_WIDTH, RET_V_WIDTH])
    c_q, c_k, c_v, c_rq, c_rk, c_rv, c_rg = (int(s) for s in starts)

    cos = cos_ref[...]
    sin = sin_ref[...]
    lane = lax.broadcasted_iota(jnp.int32, (1, LANES), 1)
    first_half = (lane % RET_QK_DIM) < RET_QK_DIM // 2

    def rotary(t, scale):
        parts = []
        for p in range(HEAD_PAIRS_RET):
            tp = t[:, p * LANES:(p + 1) * LANES]
            swapped = jnp.where(first_half,
                                pltpu.roll(tp, LANES - RET_QK_DIM // 2, 1),
                                pltpu.roll(tp, RET_QK_DIM // 2, 1))
            parts.append((tp * cos + swapped * sin) * scale)
        return jnp.concatenate(parts, axis=-1)

    rqk = proj(c_rq, c_rk + RET_QK_WIDTH)
    rq = rotary(rqk[:, :RET_QK_WIDTH], 1.0)
    rq_ref[...] = rq.astype(BF16)
    qf_ref[...] = decayed(rq, 0)
    qb_ref[...] = decayed(rq, 1)
    rk = rotary(rqk[:, RET_QK_WIDTH:], RET_QK_DIM ** -0.5)
    rk_ref[...] = rk.astype(BF16)
    kf_ref[...] = decayed(rk, 2)
    kb_ref[...] = decayed(rk, 3)
    g = proj(c_rg, c_rg + RET_V_WIDTH)
    rg_ref[...] = (g / (1.0 + jnp.exp(-g))).astype(BF16)
    rv_ref[...] = proj(c_rv, c_rv + RET_V_WIDTH).astype(BF16)
    q_ref[...] = (proj(c_q, c_q + NA_WIDTH) * NA_HEAD_DIM ** -0.5).astype(BF16)
    k_ref[...] = proj(c_k, c_k + NA_WIDTH).astype(BF16)
    v_ref[...] = proj(c_v, c_v + NA_WIDTH).astype(BF16)


def _in_proj(layer, lgf, lgb, x2, gains, w, cos_t, sin_t, seq):
    n, d = x2.shape
    tm = TM_PROJ
    pos_blocks = seq // tm
    tok = lambda i: (i, 0)
    smem = pl.BlockSpec(memory_space=pltpu.SMEM)
    widths = (NA_WIDTH,) * 3 + (RET_QK_WIDTH,) * 6 + (RET_V_WIDTH,) * 2
    return pl.pallas_call(
        functools.partial(_in_proj_kernel, layer=layer),
        out_shape=[jax.ShapeDtypeStruct((n, w_), BF16) for w_ in widths],
        grid=(n // tm,),
        in_specs=[
            smem, smem,
            pl.BlockSpec((tm, d), tok),
            _layer_block(gains, layer, 1),
            _layer_block(w, layer, 1, pipeline_mode=pl.Buffered(1)),
            pl.BlockSpec((tm, LANES), lambda i: (i % pos_blocks, 0)),
            pl.BlockSpec((tm, LANES), lambda i: (i % pos_blocks, 0)),
        ],
        out_specs=[pl.BlockSpec((tm, w_), tok) for w_ in widths],
        scratch_shapes=[pltpu.VMEM((4, RET_CHUNK, RET_QK_WIDTH), F32)],
        compiler_params=pltpu.CompilerParams(
            dimension_semantics=("arbitrary",), vmem_limit_bytes=VMEM_LIMIT),
        name="in_proj",
    )(lgf, lgb, x2, gains, w, cos_t, sin_t)


def _natten_bias_table(rpb):
    depth = rpb.shape[0]
    qc = np.arange(GRID_W)[:, None]
    kc = np.arange(GRID_W)[None, :]
    dc = np.clip(kc - qc, -(NA_COLS - 1), NA_COLS - 1) + NA_COLS - 1
    lo = np.clip(qc - NA_COLS // 2, 0, GRID_W - NA_COLS)
    valid = (kc >= lo) & (kc < lo + NA_COLS)
    onehot = (dc[None] == np.arange(2 * NA_COLS - 1)[:, None, None]).astype(np.float32)
    t = jnp.einsum('lhdc,cqk->lhdqk', rpb.astype(F32), jnp.asarray(onehot),
                   precision=lax.Precision.HIGHEST)
    t = jnp.where(jnp.asarray(valid), t, NEG)

    def row_pairs(u):
        return jnp.concatenate([u[:, :, :-1], u[:, :, 1:]], axis=-1)

    n_off = 2 * NA_ROWS - 2
    t2 = row_pairs(t).reshape(depth, HEAD_PAIRS_NA, 2, n_off, GRID_W, 2 * GRID_W)
    return jnp.transpose(t2, (0, 1, 3, 2, 4, 5)).reshape(
        depth, HEAD_PAIRS_NA, n_off, 2 * GRID_W, 2 * GRID_W)


def _chunk_kv(k_decayed, v_ref, rows, p):
    v2 = v_ref[0, rows, 2 * p * RET_V_DIM:(2 * p + 2) * RET_V_DIM]
    t = lax.dot_general(k_decayed, v2, _TN, preferred_element_type=F32)
    return jnp.where(_second_head_rows(), t[:, RET_V_DIM:], t[:, :RET_V_DIM])


def _per_head_states(state):
    second = _second_head_rows()
    return (jnp.where(second, 0.0, state).astype(BF16), jnp.where(second, state, 0.0).astype(BF16))


def _ret_bwd_kernel(lgb_ref, kb_ref, v_ref, sb_ref, state_ref, *, layer):
    @pl.when(pl.program_id(1) == 0)
    def _():
        state_ref[...] = jnp.zeros_like(state_ref)

    c = RET_CHUNK
    chunks = range(RET_BWD_CHUNKS_PER_STEP)
    pairs = range(HEAD_PAIRS_RET)
    upd = {}
    for p in pairs:
        for cc in chunks:
            rows = slice(cc * c, (cc + 1) * c)
            upd[cc, p] = _chunk_kv(kb_ref[0, rows, p * LANES:(p + 1) * LANES], v_ref, rows, p)
    for p in pairs:
        decay = jnp.exp(_pair_row_values(lgb_ref, layer, p) * float(c))
        state = state_ref[p]
        for cc in reversed(chunks):
            sb_ref[0, cc, p] = state.astype(BF16)
            state = decay * state + upd[cc, p]
        state_ref[p] = state


def _ret_bwd(layer, lgb, kb, rv):
    b, s, _ = kb.shape
    nch = RET_BWD_CHUNKS_PER_STEP
    tc = nch * RET_CHUNK
    steps = s // tc
    rev = lambda i, j: (i, steps - 1 - j, 0)
    return pl.pallas_call(
        functools.partial(_ret_bwd_kernel, layer=layer),
        out_shape=jax.ShapeDtypeStruct((b, s // RET_CHUNK, HEAD_PAIRS_RET, LANES, RET_V_DIM), BF16),
        grid=(b, steps),
        in_specs=[
            pl.BlockSpec(memory_space=pltpu.SMEM),
            pl.BlockSpec((1, tc, RET_QK_WIDTH), rev),
            pl.BlockSpec((1, tc, RET_V_WIDTH), rev),
        ],
        out_specs=pl.BlockSpec((1, nch, HEAD_PAIRS_RET, LANES, RET_V_DIM),
                               lambda i, j: (i, steps - 1 - j, 0, 0, 0)),
        scratch_shapes=[pltpu.VMEM((HEAD_PAIRS_RET, LANES, RET_V_DIM), F32)],
        compiler_params=pltpu.CompilerParams(
            dimension_semantics=("parallel", "arbitrary"), vmem_limit_bytes=VMEM_LIMIT),
        name="ret_bwd",
    )(lgb, kb, rv)


def _ret_fwd_kernel(lgf_ref, lgb_ref, q_ref, k_ref, qf_ref, qb_ref, kf_ref, v_ref, gate_ref, sb_ref,
                    gain_ref, o_ref, state_ref, dmat_ref, *, layer):
    c = RET_CHUNK

    @pl.when(pl.program_id(1) == 0)
    def _():
        state_ref[...] = jnp.zeros_like(state_ref)
        i = lax.broadcasted_iota(jnp.int32, (c, c), 0)
        j = lax.broadcasted_iota(jnp.int32, (c, c), 1)
        diff = (i - j).astype(F32)
        for h in range(RET_HEADS):
            dmat_ref[h // 2, (h % 2) * c:(h % 2 + 1) * c, :] = jnp.exp(
                jnp.where(diff >= 0, lgf_ref[layer, h] * diff, -lgb_ref[layer, h] * diff))

    second = _lane_is_second_head()
    chunks = range(RET_FWD_CHUNKS_PER_STEP)
    pairs = range(HEAD_PAIRS_RET)
    rows = [slice(cc * c, (cc + 1) * c) for cc in chunks]
    cols = [slice(p * LANES, (p + 1) * LANES) for p in pairs]

    upd = {(cc, p): _chunk_kv(kf_ref[0, rows[cc], cols[p]], v_ref, rows[cc], p)
           for cc in chunks for p in pairs}

    scores = {}
    for cc in chunks:
        for p in pairs:
            qp = q_ref[0, rows[cc], cols[p]]
            q2 = jnp.concatenate([jnp.where(second, jnp.zeros_like(qp), qp),
                                  jnp.where(second, qp, jnp.zeros_like(qp))], axis=0)
            s = lax.dot_general(q2, k_ref[0, rows[cc], cols[p]], _NT, preferred_element_type=F32)
            scores[cc, p] = (s * dmat_ref[p]).astype(BF16)

    before = {}
    for p in pairs:
        decay = jnp.exp(_pair_row_values(lgf_ref, layer, p) * float(c))
        state = state_ref[p]
        for cc in chunks:
            before[cc, 2 * p], before[cc, 2 * p + 1] = _per_head_states(state)
            state = decay * state + upd[cc, p]
        state_ref[p] = state

    second_rows = _second_head_rows()
    for cc in chunks:
        for p in pairs:
            qf = qf_ref[0, rows[cc], cols[p]]
            qb = qb_ref[0, rows[cc], cols[p]]
            sb = sb_ref[0, cc, p]
            after = (jnp.where(second_rows, jnp.zeros_like(sb), sb), jnp.where(second_rows, sb, jnp.zeros_like(sb)))
            for hh in range(2):
                h = 2 * p + hh
                vcols = slice(h * RET_V_DIM, (h + 1) * RET_V_DIM)
                lhs = jnp.concatenate([scores[cc, p][hh * c:(hh + 1) * c], qf, qb], axis=-1)
                rhs = jnp.concatenate([v_ref[0, rows[cc], vcols], before[cc, h], after[hh]], axis=0)
                y = jnp.dot(lhs, rhs, preferred_element_type=F32)
                mu = jnp.mean(y, axis=-1, keepdims=True)
                yc = y - mu
                var = jnp.mean(yc * yc, axis=-1, keepdims=True)
                yn = yc * lax.rsqrt(var + EPS) * gain_ref[:, vcols]
                o_ref[0, rows[cc], vcols] = (gate_ref[0, rows[cc], vcols].astype(F32) * yn).astype(BF16)


def _ret_fwd(layer, lgf, lgb, rq, rk, qf, qb, kf, rv, gate, sb, gains):
    b, s, _ = rq.shape
    nch = RET_FWD_CHUNKS_PER_STEP
    tc = nch * RET_CHUNK
    tok = lambda i, j: (i, j, 0)
    smem = pl.BlockSpec(memory_space=pltpu.SMEM)
    qk_spec = pl.BlockSpec((1, tc, RET_QK_WIDTH), tok)
    v_spec = pl.BlockSpec((1, tc, RET_V_WIDTH), tok)
    return pl.pallas_call(
        functools.partial(_ret_fwd_kernel, layer=layer),
        out_shape=jax.ShapeDtypeStruct((b, s, RET_V_WIDTH), BF16),
        grid=(b, s // tc),
        in_specs=[
            smem, smem,
            qk_spec, qk_spec, qk_spec, qk_spec, qk_spec, v_spec, v_spec,
            pl.BlockSpec((1, nch, HEAD_PAIRS_RET, LANES, RET_V_DIM), lambda i, j: (i, j, 0, 0, 0)),
            _layer_block(gains, layer, 2),
        ],
        out_specs=pl.BlockSpec((1, tc, RET_V_WIDTH), tok),
        scratch_shapes=[
            pltpu.VMEM((HEAD_PAIRS_RET, LANES, RET_V_DIM), F32),
            pltpu.VMEM((HEAD_PAIRS_RET, 2 * RET_CHUNK, RET_CHUNK), F32),
        ],
        compiler_params=pltpu.CompilerParams(
            dimension_semantics=("arbitrary", "arbitrary"), vmem_limit_bytes=VMEM_LIMIT),
        name="ret_fwd",
    )(lgf, lgb, rq, rk, qf, qb, kf, rv, gate, sb, gains)


def _natten_mlp_kernel(q_ref, k_prev_ref, k_cur_ref, k_next_ref, v_prev_ref, v_cur_ref, v_next_ref, bias_ref,
                       x_ref, ret_ref, wo_ref, gain_ref, wu_ref, wd_ref, gfin_ref,
                       o_ref, na_ref, kwin_ref, vwin_ref, *, rows, tiles_per_batch, n_tiles, final_norm):
    step = pl.program_id(0)
    tm = NA_ROWS_PER_STEP * GRID_W

    def load_windows():
        for win_ref, parts in ((kwin_ref, (k_prev_ref, k_cur_ref, k_next_ref)),
                               (vwin_ref, (v_prev_ref, v_cur_ref, v_next_ref))):
            for i, part in enumerate(parts):
                win_ref[i * tm:(i + 1) * tm, :] = part[0]

    second = _lane_is_second_head()
    pairs = range(HEAD_PAIRS_NA)
    cols = [slice(p * LANES, (p + 1) * LANES) for p in pairs]
    keys = NA_ROWS * GRID_W
    tile = jnp.minimum(step, n_tiles - 1)
    r0 = (tile % tiles_per_batch) * NA_ROWS_PER_STEP

    def offsets(rr):
        r = r0 + rr
        rs = jnp.clip(r - NA_ROWS // 2, 0, rows - NA_ROWS)
        d0 = rs - r + NA_ROWS - 1
        return rr * GRID_W, pl.multiple_of((rs - r0 + NA_ROWS_PER_STEP) * GRID_W, GRID_W), d0

    def row_scores(rr):
        qoff, koff, d0 = offsets(rr)
        scores = []
        for p in pairs:
            qp = q_ref[qoff:qoff + GRID_W, cols[p]]
            q2 = jnp.concatenate([jnp.where(second, jnp.zeros_like(qp), qp),
                                  jnp.where(second, qp, jnp.zeros_like(qp))], axis=0)
            kp = kwin_ref[pl.ds(koff, keys), cols[p]]
            bias = jnp.concatenate([bias_ref[p, d0 + 2 * j] for j in range(NA_ROWS // 2)], axis=-1)
            scores.append(lax.dot_general(q2, kp, _NT, preferred_element_type=F32) + bias)
        return scores

    def softmax_terms(scores):
        terms = []
        for s in scores:
            e = jnp.exp(s - jnp.max(s, axis=-1, keepdims=True))
            terms.append((e.astype(BF16), jnp.sum(e, axis=-1, keepdims=True)))
        return terms

    def row_outputs(rr, terms):
        qoff, koff, _ = offsets(rr)
        for p in pairs:
            e, denom = terms[p]
            vp = vwin_ref[pl.ds(koff, keys), cols[p]]
            o2 = jnp.dot(e, vp, preferred_element_type=F32) / denom
            out = jnp.where(second, o2[GRID_W:], o2[:GRID_W])
            na_ref[qoff:qoff + GRID_W, cols[p]] = out.astype(BF16)

    d_ff = wu_ref.shape[1]
    n_chunks = d_ff // FF_CHUNK
    rows_per_chunk = NA_ROWS_PER_STEP // n_chunks
    chunk_rows = lambda c: range(c * rows_per_chunk, (c + 1) * rows_per_chunk)

    def body(attend, dense):
        if dense:
            mix = (jnp.dot(na_ref[...], wo_ref[:NA_WIDTH, :], preferred_element_type=F32)
                   + jnp.dot(ret_ref[...], wo_ref[NA_WIDTH:, :], preferred_element_type=F32))
            x1 = x_ref[...] + mix
            h = _rms(x1, gain_ref[...]).astype(BF16)
        if attend:
            load_windows()
            scores = {rr: row_scores(rr) for rr in chunk_rows(0)}
        mlp = None
        for c in range(n_chunks):
            f0 = c * FF_CHUNK
            if dense:
                u = jnp.maximum(jnp.dot(h, wu_ref[:, f0:f0 + FF_CHUNK], preferred_element_type=F32), 0.0)
            if attend:
                for rr in chunk_rows(c):
                    row_outputs(rr, softmax_terms(scores.pop(rr)))
                if c + 1 < n_chunks:
                    for rr in chunk_rows(c + 1):
                        scores[rr] = row_scores(rr)
            if dense:
                t = jnp.dot((u * u).astype(BF16), wd_ref[f0:f0 + FF_CHUNK, :], preferred_element_type=F32)
                mlp = t if mlp is None else mlp + t
        if dense:
            x3 = x1 + mlp
            o_ref[...] = _rms(x3, gfin_ref[...]) if final_norm else x3

    pl.when(step == 0)(functools.partial(body, True, False))
    pl.when(jnp.logical_and(step > 0, step < n_tiles))(functools.partial(body, True, True))
    pl.when(step == n_tiles)(functools.partial(body, False, True))


def _natten_mlp(layer, q, k, v, bias, x2, ret, wo, gains, wu, wd, gfin, final_norm):
    n, d = x2.shape
    b, s, w = k.shape
    rows = s // GRID_W
    tm = NA_ROWS_PER_STEP * GRID_W
    n_tiles = n // tm
    tiles_per_batch = s // tm
    att = lambda i: jnp.minimum(i, n_tiles - 1)
    prev = lambda i: jnp.maximum(i - 1, 0)
    resident = lambda a: _layer_block(a, layer, 1, pipeline_mode=pl.Buffered(1))

    def neighbour_tile(shift):
        def index_map(i):
            t = att(i)
            tb = jnp.clip(t % tiles_per_batch + shift, 0, tiles_per_batch - 1)
            return (t // tiles_per_batch, tb, 0)
        return pl.BlockSpec((1, tm, w), index_map)

    halo = [neighbour_tile(-1), neighbour_tile(0), neighbour_tile(1)]
    return pl.pallas_call(
        functools.partial(_natten_mlp_kernel, rows=rows, tiles_per_batch=tiles_per_batch,
                          n_tiles=n_tiles, final_norm=final_norm),
        out_shape=jax.ShapeDtypeStruct((n, d), F32),
        grid=(n_tiles + 1,),
        in_specs=[
            pl.BlockSpec((tm, w), lambda i: (att(i), 0)),
            *halo,
            *halo,
            resident(bias),
            pl.BlockSpec((tm, d), lambda i: (prev(i), 0)),
            pl.BlockSpec((tm, RET_V_WIDTH), lambda i: (prev(i), 0)),
            resident(wo),
            _layer_block(gains, layer, 1),
            resident(wu),
            resident(wd),
            pl.BlockSpec((1, d), lambda i: (0, 0)),
        ],
        out_specs=pl.BlockSpec((tm, d), lambda i: (prev(i), 0)),
        scratch_shapes=[pltpu.VMEM((tm, NA_WIDTH), BF16),
                        pltpu.VMEM((3 * tm, NA_WIDTH), BF16),
                        pltpu.VMEM((3 * tm, NA_WIDTH), BF16)],
        compiler_params=pltpu.CompilerParams(
            dimension_semantics=("arbitrary",), vmem_limit_bytes=VMEM_LIMIT),
        name="natten_mlp",
    )(q, k, k, k, v, v, v, bias, x2, ret, wo, gains, wu, wd, gfin)


def _rotary_tables(seq):
    half = RET_QK_DIM // 2
    inv = 1.0 / (ROPE_BASE ** (jnp.arange(0, RET_QK_DIM, 2, dtype=F32) / RET_QK_DIM))
    ang = jnp.arange(seq).astype(F32)[:, None] * inv[None, :]
    cos = jnp.cos(ang)
    sin = jnp.sin(ang)
    cos_t = jnp.tile(cos, (1, LANES // half))
    sin_t = jnp.tile(jnp.concatenate([-sin, sin], axis=-1), (1, LANES // RET_QK_DIM))
    return cos_t, sin_t


def kernel(x, w_in, w_out, na_rpb, ret_decay_fwd, ret_decay_bwd, ret_norm_gain,
           norm_mix, norm_mlp, w_up, w_down, norm_final):
    bsz, seq, d = x.shape
    depth = w_in.shape[0]
    n = bsz * seq
    rows = seq // GRID_W
    assert TM_PROJ % RET_CHUNK == 0 and seq % TM_PROJ == 0
    assert seq % (RET_BWD_CHUNKS_PER_STEP * RET_CHUNK) == 0
    assert seq % (RET_FWD_CHUNKS_PER_STEP * RET_CHUNK) == 0
    assert rows % NA_ROWS_PER_STEP == 0 and rows >= NA_ROWS and NA_ROWS_PER_STEP >= NA_ROWS // 2
    assert w_up.shape[-1] % FF_CHUNK == 0 and NA_ROWS_PER_STEP % (w_up.shape[-1] // FF_CHUNK) == 0

    cos_t, sin_t = _rotary_tables(seq)
    w_in_b, w_out_b, w_up_b, w_down_b = (w.astype(BF16) for w in (w_in, w_out, w_up, w_down))
    lgf = jnp.log1p(-jnp.exp(ret_decay_fwd.astype(F32)))
    lgb = jnp.log1p(-jnp.exp(ret_decay_bwd.astype(F32)))
    bias = _natten_bias_table(na_rpb)
    g_mix = norm_mix.astype(F32).reshape(depth, 1, d)
    g_mlp = norm_mlp.astype(F32).reshape(depth, 1, d)
    g_ret = ret_norm_gain.astype(F32).reshape(depth, 1, RET_V_WIDTH)
    g_fin = norm_final.astype(F32).reshape(1, d)

    x2 = x.reshape(n, d)
    as_seq = lambda t: t.reshape(bsz, seq, t.shape[-1])
    for l in range(depth):
        q, k, v, rq, rk, qf, qb, kf, kb, rv, gate = _in_proj(
            l, lgf, lgb, x2, g_mix, w_in_b, cos_t, sin_t, seq)
        sb = _ret_bwd(l, lgb, as_seq(kb), as_seq(rv))
        ret = _ret_fwd(l, lgf, lgb, as_seq(rq), as_seq(rk), as_seq(qf), as_seq(qb), as_seq(kf),
                       as_seq(rv), as_seq(gate), sb, g_ret)
        x2 = _natten_mlp(l, q, as_seq(k), as_seq(v), bias, x2, ret.reshape(n, RET_V_WIDTH),
                         w_out_b, g_mlp, w_up_b, w_down_b, g_fin, l == depth - 1)
    return x2.reshape(bsz, seq, d)
```

```python
import functools

import jax
import jax.numpy as jnp
import numpy as np
from jax import lax
from jax.experimental import pallas as pl
from jax.experimental.pallas import tpu as pltpu

F32 = jnp.float32
BF16 = jnp.bfloat16

GRID_W = 64
NA_HEADS = 8
NA_HEAD_DIM = 64
NA_WIDTH = NA_HEADS * NA_HEAD_DIM
NA_ROWS = 8
NA_COLS = 16
RET_HEADS = 4
RET_QK_DIM = 64
RET_V_DIM = 128
RET_QK_WIDTH = RET_HEADS * RET_QK_DIM
RET_V_WIDTH = RET_HEADS * RET_V_DIM
RET_CHUNK = 128
ROPE_BASE = 10000.0
EPS = 1e-6
NEG = -1e30

LANES = 128
HEAD_PAIRS_NA = NA_WIDTH // LANES
HEAD_PAIRS_RET = RET_QK_WIDTH // LANES
VMEM_LIMIT = 60 * 1024 * 1024

TM_PROJ = 1024
FF_CHUNK = 1024
NA_ROWS_PER_STEP = 8
RET_BWD_CHUNKS_PER_STEP = 16

_NT = (((1,), (1,)), ((), ()))
_TN = (((0,), (0,)), ((), ()))


def _rms(x, gain):
    return x * lax.rsqrt(jnp.mean(x * x, axis=-1, keepdims=True) + EPS) * gain


def _lane_is_second_head():
    return lax.broadcasted_iota(jnp.int32, (1, LANES), 1) >= NA_HEAD_DIM


def _pair_lane_values(smem_ref, layer, p):
    return jnp.where(_lane_is_second_head(), smem_ref[layer, 2 * p + 1], smem_ref[layer, 2 * p])


def _second_head_rows():
    return lax.broadcasted_iota(jnp.int32, (LANES, 1), 0) >= RET_QK_DIM


def _pair_row_values(smem_ref, layer, p):
    return jnp.where(_second_head_rows(), smem_ref[layer, 2 * p + 1], smem_ref[layer, 2 * p])


def _layer_block(a, layer, grid_rank, **kw):
    zeros = (0,) * (a.ndim - 1)
    if grid_rank == 1:
        index_map = lambda i: (layer,) + zeros
    else:
        index_map = lambda i, j: (layer,) + zeros
    return pl.BlockSpec((None,) + a.shape[1:], index_map, **kw)


def _in_proj_kernel(lgf_ref, lgb_ref, x_ref, gain_ref, w_ref, cos_ref, sin_ref,
                    q_ref, k_ref, v_ref, rq_ref, rk_ref, qf_ref, qb_ref, kf_ref, kb_ref, rv_ref, rg_ref,
                    decay_ref, *, layer):
    c = RET_CHUNK

    @pl.when(pl.program_id(0) == 0)
    def _():
        pos = lax.broadcasted_iota(jnp.int32, (c, 1), 0).astype(F32)
        for p in range(HEAD_PAIRS_RET):
            lanes = slice(p * LANES, (p + 1) * LANES)
            lf = _pair_lane_values(lgf_ref, layer, p)
            lb = _pair_lane_values(lgb_ref, layer, p)
            decay_ref[0, :, lanes] = jnp.exp(lf * (pos + 1.0))
            decay_ref[1, :, lanes] = jnp.exp(lb * (float(c) - pos))
            decay_ref[2, :, lanes] = jnp.exp(lf * (float(c) - 1.0 - pos))
            decay_ref[3, :, lanes] = jnp.exp(lb * pos)

    h = _rms(x_ref[...], gain_ref[...]).astype(BF16)
    tm = h.shape[0]

    def decayed(t, which):
        t3 = t.reshape(tm // c, c, RET_QK_WIDTH) * decay_ref[which][None]
        return t3.reshape(tm, RET_QK_WIDTH).astype(BF16)

    def proj(c0, c1):
        return jnp.dot(h, w_ref[:, c0:c1], preferred_element_type=F32)

    starts = np.cumsum([0, NA_WIDTH, NA_WIDTH, NA_WIDTH, RET_QK_WIDTH, RET_QK_WIDTH, RET_V_WIDTH])
    c_q, c_k, c_v, c_rq, c_rk, c_rv, c_rg = (int(s) for s in starts)

    cos = cos_ref[...]
    sin = sin_ref[...]
    lane = lax.broadcasted_iota(jnp.int32, (1, LANES), 1)
    first_half = (lane % RET_QK_DIM) < RET_QK_DIM // 2

    def rotary(t, scale):
        parts = []
        for p in range(HEAD_PAIRS_RET):
            tp = t[:, p * LANES:(p + 1) * LANES]
            swapped = jnp.where(first_half,
                                pltpu.roll(tp, LANES - RET_QK_DIM // 2, 1),
                                pltpu.roll(tp, RET_QK_DIM // 2, 1))
            parts.append((tp * cos + swapped * sin) * scale)
        return jnp.concatenate(parts, axis=-1)

    rqk = proj(c_rq, c_rk + RET_QK_WIDTH)
    rq = rotary(rqk[:, :RET_QK_WIDTH], 1.0)
    rq_ref[...] = rq.astype(BF16)
    qf_ref[...] = decayed(rq, 0)
    qb_ref[...] = decayed(rq, 1)
    rk = rotary(rqk[:, RET_QK_WIDTH:], RET_QK_DIM ** -0.5)
    rk_ref[...] = rk.astype(BF16)
    kf_ref[...] = decayed(rk, 2)
    kb_ref[...] = decayed(rk, 3)
    g = proj(c_rg, c_rg + RET_V_WIDTH)
    rg_ref[...] = (g / (1.0 + jnp.exp(-g))).astype(BF16)
    rv_ref[...] = proj(c_rv, c_rv + RET_V_WIDTH).astype(BF16)
    q_ref[...] = (proj(c_q, c_q + NA_WIDTH) * NA_HEAD_DIM ** -0.5).astype(BF16)
    k_ref[...] = proj(c_k, c_k + NA_WIDTH).astype(BF16)
    v_ref[...] = proj(c_v, c_v + NA_WIDTH).astype(BF16)


def _in_proj(layer, lgf, lgb, x2, gains, w, cos_t, sin_t, seq):
    n, d = x2.shape
    tm = TM_PROJ
    pos_blocks = seq // tm
    tok = lambda i: (i, 0)
    smem = pl.BlockSpec(memory_space=pltpu.SMEM)
    widths = (NA_WIDTH,) * 3 + (RET_QK_WIDTH,) * 6 + (RET_V_WIDTH,) * 2
    return pl.pallas_call(
        functools.partial(_in_proj_kernel, layer=layer),
        out_shape=[jax.ShapeDtypeStruct((n, w_), BF16) for w_ in widths],
        grid=(n // tm,),
        in_specs=[
            smem, smem,
            pl.BlockSpec((tm, d), tok),
            _layer_block(gains, layer, 1),
            _layer_block(w, layer, 1, pipeline_mode=pl.Buffered(1)),
            pl.BlockSpec((tm, LANES), lambda i: (i % pos_blocks, 0)),
            pl.BlockSpec((tm, LANES), lambda i: (i % pos_blocks, 0)),
        ],
        out_specs=[pl.BlockSpec((tm, w_), tok) for w_ in widths],
        scratch_shapes=[pltpu.VMEM((4, RET_CHUNK, RET_QK_WIDTH), F32)],
        compiler_params=pltpu.CompilerParams(
            dimension_semantics=("arbitrary",), vmem_limit_bytes=VMEM_LIMIT),
        name="in_proj",
    )(lgf, lgb, x2, gains, w, cos_t, sin_t)


def _natten_bias_table(rpb):
    depth = rpb.shape[0]
    qc = np.arange(GRID_W)[:, None]
    kc = np.arange(GRID_W)[None, :]
    dc = np.clip(kc - qc, -(NA_COLS - 1), NA_COLS - 1) + NA_COLS - 1
    lo = np.clip(qc - NA_COLS // 2, 0, GRID_W - NA_COLS)
    valid = (kc >= lo) & (kc < lo + NA_COLS)
    onehot = (dc[None] == np.arange(2 * NA_COLS - 1)[:, None, None]).astype(np.float32)
    t = jnp.einsum('lhdc,cqk->lhdqk', rpb.astype(F32), jnp.asarray(onehot),
                   precision=lax.Precision.HIGHEST)
    t = jnp.where(jnp.asarray(valid), t, NEG)

    def row_pairs(u):
        return jnp.concatenate([u[:, :, :-1], u[:, :, 1:]], axis=-1)

    n_off = 2 * NA_ROWS - 2
    t2 = row_pairs(t).reshape(depth, HEAD_PAIRS_NA, 2, n_off, GRID_W, 2 * GRID_W)
    return jnp.transpose(t2, (0, 1, 3, 2, 4, 5)).reshape(
        depth, HEAD_PAIRS_NA, n_off, 2 * GRID_W, 2 * GRID_W)


def _chunk_kv(k_decayed, v2):
    t = lax.dot_general(k_decayed, v2, _TN, preferred_element_type=F32)
    return jnp.where(_second_head_rows(), t[:, RET_V_DIM:], t[:, :RET_V_DIM])


def _per_head_states(state):
    second = _second_head_rows()
    return (jnp.where(second, 0.0, state).astype(BF16), jnp.where(second, state, 0.0).astype(BF16))


def _ret_bwd_kernel(lgb_ref, kb_ref, v_ref, sb_ref, state_ref, *, layer):
    @pl.when(pl.program_id(1) == 0)
    def _():
        state_ref[...] = jnp.zeros_like(state_ref)

    c = RET_CHUNK
    chunks = range(RET_BWD_CHUNKS_PER_STEP)
    pairs = range(HEAD_PAIRS_RET)
    upd = {}
    for p in pairs:
        for cc in chunks:
            rows = slice(cc * c, (cc + 1) * c)
            upd[cc, p] = _chunk_kv(kb_ref[0, rows, p * LANES:(p + 1) * LANES],
                                   v_ref[0, rows, 2 * p * RET_V_DIM:(2 * p + 2) * RET_V_DIM])
    for p in pairs:
        decay = jnp.exp(_pair_row_values(lgb_ref, layer, p) * float(c))
        state = state_ref[p]
        for cc in reversed(chunks):
            sb_ref[0, cc, p] = state.astype(BF16)
            state = decay * state + upd[cc, p]
        state_ref[p] = state


def _ret_bwd(layer, lgb, kb, rv):
    b, s, _ = kb.shape
    nch = RET_BWD_CHUNKS_PER_STEP
    tc = nch * RET_CHUNK
    steps = s // tc
    rev = lambda i, j: (i, steps - 1 - j, 0)
    return pl.pallas_call(
        functools.partial(_ret_bwd_kernel, layer=layer),
        out_shape=jax.ShapeDtypeStruct((b, s // RET_CHUNK, HEAD_PAIRS_RET, LANES, RET_V_DIM), BF16),
        grid=(b, steps),
        in_specs=[
            pl.BlockSpec(memory_space=pltpu.SMEM),
            pl.BlockSpec((1, tc, RET_QK_WIDTH), rev),
            pl.BlockSpec((1, tc, RET_V_WIDTH), rev),
        ],
        out_specs=pl.BlockSpec((1, nch, HEAD_PAIRS_RET, LANES, RET_V_DIM),
                               lambda i, j: (i, steps - 1 - j, 0, 0, 0)),
        scratch_shapes=[pltpu.VMEM((HEAD_PAIRS_RET, LANES, RET_V_DIM), F32)],
        compiler_params=pltpu.CompilerParams(
            dimension_semantics=("parallel", "arbitrary"), vmem_limit_bytes=VMEM_LIMIT),
        name="ret_bwd",
    )(lgb, kb, rv)


def _ret_decay_matrix(lgf_ref, lgb_ref, dmat_ref, layer):
    c = RET_CHUNK
    i = lax.broadcasted_iota(jnp.int32, (c, c), 0)
    j = lax.broadcasted_iota(jnp.int32, (c, c), 1)
    diff = (i - j).astype(F32)
    for h in range(RET_HEADS):
        dmat_ref[h // 2, (h % 2) * c:(h % 2 + 1) * c, :] = jnp.exp(
            jnp.where(diff >= 0, lgf_ref[layer, h] * diff, -lgb_ref[layer, h] * diff))


def _ret_scores_and_states(n_chunks, q_ref, k_ref, kf_ref, v_ref, lgf_ref, state_ref, dmat_ref, layer):
    c = RET_CHUNK
    second = _lane_is_second_head()
    chunks = range(n_chunks)
    pairs = range(HEAD_PAIRS_RET)
    rows = [slice(cc * c, (cc + 1) * c) for cc in chunks]
    cols = [slice(p * LANES, (p + 1) * LANES) for p in pairs]
    upd = {(cc, p): _chunk_kv(kf_ref[rows[cc], cols[p]],
                              v_ref[rows[cc], 2 * p * RET_V_DIM:(2 * p + 2) * RET_V_DIM])
           for cc in chunks for p in pairs}
    scores = {}
    for cc in chunks:
        for p in pairs:
            qp = q_ref[rows[cc], cols[p]]
            q2 = jnp.concatenate([jnp.where(second, jnp.zeros_like(qp), qp),
                                  jnp.where(second, qp, jnp.zeros_like(qp))], axis=0)
            s = lax.dot_general(q2, k_ref[rows[cc], cols[p]], _NT, preferred_element_type=F32)
            scores[cc, p] = (s * dmat_ref[p]).astype(BF16)
    before = {}
    for p in pairs:
        decay = jnp.exp(_pair_row_values(lgf_ref, layer, p) * float(c))
        state = state_ref[p]
        for cc in chunks:
            before[cc, 2 * p], before[cc, 2 * p + 1] = _per_head_states(state)
            state = decay * state + upd[cc, p]
        state_ref[p] = state
    return scores, before


def _ret_outputs(n_chunks, scores, before, qf_ref, qb_ref, v_ref, gate_ref, sb_ref, gain_ref, o_ref):
    c = RET_CHUNK
    second_rows = _second_head_rows()
    for cc in range(n_chunks):
        rows = slice(cc * c, (cc + 1) * c)
        for p in range(HEAD_PAIRS_RET):
            cols = slice(p * LANES, (p + 1) * LANES)
            qf = qf_ref[rows, cols]
            qb = qb_ref[rows, cols]
            sb = sb_ref[cc, p]
            after = (jnp.where(second_rows, jnp.zeros_like(sb), sb), jnp.where(second_rows, sb, jnp.zeros_like(sb)))
            for hh in range(2):
                h = 2 * p + hh
                vcols = slice(h * RET_V_DIM, (h + 1) * RET_V_DIM)
                lhs = jnp.concatenate([scores[cc, p][hh * c:(hh + 1) * c], qf, qb], axis=-1)
                rhs = jnp.concatenate([v_ref[rows, vcols], before[cc, h], after[hh]], axis=0)
                y = jnp.dot(lhs, rhs, preferred_element_type=F32)
                mu = jnp.mean(y, axis=-1, keepdims=True)
                yc = y - mu
                var = jnp.mean(yc * yc, axis=-1, keepdims=True)
                yn = yc * lax.rsqrt(var + EPS) * gain_ref[:, vcols]
                o_ref[rows, vcols] = (gate_ref[rows, vcols].astype(F32) * yn).astype(BF16)


def _mixers_mlp_kernel(lgf_ref, lgb_ref,
                       q_ref, k_prev_ref, k_cur_ref, k_next_ref, v_prev_ref, v_cur_ref, v_next_ref, bias_ref,
                       rq_ref, rk_ref, qf_ref, qb_ref, kf_ref, rv_ref, gate_ref, sb_ref, rgain_ref,
                       x_ref, wo_ref, gain_ref, wu_ref, wd_ref, gfin_ref,
                       o_ref, na_ref, ret_ref, kwin_ref, vwin_ref, state_ref, dmat_ref,
                       *, layer, rows, tiles_per_batch, n_tiles, final_norm):
    step = pl.program_id(0)
    tm = NA_ROWS_PER_STEP * GRID_W
    ret_chunks = tm // RET_CHUNK

    @pl.when(step == 0)
    def _():
        _ret_decay_matrix(lgf_ref, lgb_ref, dmat_ref, layer)

    @pl.when(step % tiles_per_batch == 0)
    def _():
        state_ref[...] = jnp.zeros_like(state_ref)

    def load_windows():
        for win_ref, parts in ((kwin_ref, (k_prev_ref, k_cur_ref, k_next_ref)),
                               (vwin_ref, (v_prev_ref, v_cur_ref, v_next_ref))):
            for i, part in enumerate(parts):
                win_ref[i * tm:(i + 1) * tm, :] = part[0]

    second = _lane_is_second_head()
    pairs = range(HEAD_PAIRS_NA)
    cols = [slice(p * LANES, (p + 1) * LANES) for p in pairs]
    keys = NA_ROWS * GRID_W
    tile = jnp.minimum(step, n_tiles - 1)
    r0 = (tile % tiles_per_batch) * NA_ROWS_PER_STEP

    def offsets(rr):
        r = r0 + rr
        rs = jnp.clip(r - NA_ROWS // 2, 0, rows - NA_ROWS)
        d0 = rs - r + NA_ROWS - 1
        return rr * GRID_W, pl.multiple_of((rs - r0 + NA_ROWS_PER_STEP) * GRID_W, GRID_W), d0

    def row_scores(rr):
        qoff, koff, d0 = offsets(rr)
        scores = []
        for p in pairs:
            qp = q_ref[qoff:qoff + GRID_W, cols[p]]
            q2 = jnp.concatenate([jnp.where(second, jnp.zeros_like(qp), qp),
                                  jnp.where(second, qp, jnp.zeros_like(qp))], axis=0)
            kp = kwin_ref[pl.ds(koff, keys), cols[p]]
            bias = jnp.concatenate([bias_ref[p, d0 + 2 * j] for j in range(NA_ROWS // 2)], axis=-1)
            scores.append(lax.dot_general(q2, kp, _NT, preferred_element_type=F32) + bias)
        return scores

    def softmax_terms(scores):
        terms = []
        for s in scores:
            e = jnp.exp(s - jnp.max(s, axis=-1, keepdims=True))
            terms.append((e.astype(BF16), jnp.sum(e, axis=-1, keepdims=True)))
        return terms

    def row_outputs(rr, terms):
        qoff, koff, _ = offsets(rr)
        for p in pairs:
            e, denom = terms[p]
            vp = vwin_ref[pl.ds(koff, keys), cols[p]]
            o2 = jnp.dot(e, vp, preferred_element_type=F32) / denom
            out = jnp.where(second, o2[GRID_W:], o2[:GRID_W])
            na_ref[qoff:qoff + GRID_W, cols[p]] = out.astype(BF16)

    d_ff = wu_ref.shape[1]
    n_chunks = d_ff // FF_CHUNK
    rows_per_chunk = NA_ROWS_PER_STEP // n_chunks
    chunk_rows = lambda c: range(c * rows_per_chunk, (c + 1) * rows_per_chunk)

    def body(attend, dense):
        if dense:
            mix = (jnp.dot(na_ref[...], wo_ref[:NA_WIDTH, :], preferred_element_type=F32)
                   + jnp.dot(ret_ref[...], wo_ref[NA_WIDTH:, :], preferred_element_type=F32))
            x1 = x_ref[...] + mix
            h = _rms(x1, gain_ref[...]).astype(BF16)
        if attend:
            ret_scores, ret_before = _ret_scores_and_states(
                ret_chunks, rq_ref, rk_ref, kf_ref, rv_ref, lgf_ref, state_ref, dmat_ref, layer)
            load_windows()
            scores = {rr: row_scores(rr) for rr in chunk_rows(0)}
        mlp = None
        for c in range(n_chunks):
            f0 = c * FF_CHUNK
            if dense:
                u = jnp.maximum(jnp.dot(h, wu_ref[:, f0:f0 + FF_CHUNK], preferred_element_type=F32), 0.0)
            if attend:
                for rr in chunk_rows(c):
                    row_outputs(rr, softmax_terms(scores.pop(rr)))
                if c + 1 < n_chunks:
                    for rr in chunk_rows(c + 1):
                        scores[rr] = row_scores(rr)
            if dense:
                t = jnp.dot((u * u).astype(BF16), wd_ref[f0:f0 + FF_CHUNK, :], preferred_element_type=F32)
                mlp = t if mlp is None else mlp + t
            if attend and c == 0:
                _ret_outputs(ret_chunks, ret_scores, ret_before, qf_ref, qb_ref, rv_ref, gate_ref, sb_ref,
                             rgain_ref, ret_ref)
        if dense:
            x3 = x1 + mlp
            o_ref[...] = _rms(x3, gfin_ref[...]) if final_norm else x3

    pl.when(step == 0)(functools.partial(body, True, False))
    pl.when(jnp.logical_and(step > 0, step < n_tiles))(functools.partial(body, True, True))
    pl.when(step == n_tiles)(functools.partial(body, False, True))


def _mixers_mlp(layer, lgf, lgb, q, k, v, bias, rq, rk, qf, qb, kf, rv, gate, sb, rgains,
                x2, wo, gains, wu, wd, gfin, final_norm):
    n, d = x2.shape
    b, s, w = k.shape
    rows = s // GRID_W
    tm = NA_ROWS_PER_STEP * GRID_W
    n_tiles = n // tm
    tiles_per_batch = s // tm
    ret_chunks = tm // RET_CHUNK
    att = lambda i: jnp.minimum(i, n_tiles - 1)
    prev = lambda i: jnp.maximum(i - 1, 0)
    resident = lambda a: _layer_block(a, layer, 1, pipeline_mode=pl.Buffered(1))
    smem = pl.BlockSpec(memory_space=pltpu.SMEM)
    att_tile = lambda width: pl.BlockSpec((tm, width), lambda i: (att(i), 0))
    sb = sb.reshape((b * (s // RET_CHUNK),) + sb.shape[2:])

    def neighbour_tile(shift):
        def index_map(i):
            t = att(i)
            tb = jnp.clip(t % tiles_per_batch + shift, 0, tiles_per_batch - 1)
            return (t // tiles_per_batch, tb, 0)
        return pl.BlockSpec((1, tm, w), index_map)

    halo = [neighbour_tile(-1), neighbour_tile(0), neighbour_tile(1)]
    return pl.pallas_call(
        functools.partial(_mixers_mlp_kernel, layer=layer, rows=rows, tiles_per_batch=tiles_per_batch,
                          n_tiles=n_tiles, final_norm=final_norm),
        out_shape=jax.ShapeDtypeStruct((n, d), F32),
        grid=(n_tiles + 1,),
        in_specs=[
            smem, smem,
            att_tile(w),
            *halo,
            *halo,
            resident(bias),
            att_tile(RET_QK_WIDTH), att_tile(RET_QK_WIDTH), att_tile(RET_QK_WIDTH), att_tile(RET_QK_WIDTH),
            att_tile(RET_QK_WIDTH), att_tile(RET_V_WIDTH), att_tile(RET_V_WIDTH),
            pl.BlockSpec((ret_chunks, HEAD_PAIRS_RET, LANES, RET_V_DIM), lambda i: (att(i), 0, 0, 0)),
            _layer_block(rgains, layer, 1),
            pl.BlockSpec((tm, d), lambda i: (prev(i), 0)),
            resident(wo),
            _layer_block(gains, layer, 1),
            resident(wu),
            resident(wd),
            pl.BlockSpec((1, d), lambda i: (0, 0)),
        ],
        out_specs=pl.BlockSpec((tm, d), lambda i: (prev(i), 0)),
        scratch_shapes=[pltpu.VMEM((tm, NA_WIDTH), BF16),
                        pltpu.VMEM((tm, RET_V_WIDTH), BF16),
                        pltpu.VMEM((3 * tm, NA_WIDTH), BF16),
                        pltpu.VMEM((3 * tm, NA_WIDTH), BF16),
                        pltpu.VMEM((HEAD_PAIRS_RET, LANES, RET_V_DIM), F32),
                        pltpu.VMEM((HEAD_PAIRS_RET, 2 * RET_CHUNK, RET_CHUNK), F32)],
        compiler_params=pltpu.CompilerParams(
            dimension_semantics=("arbitrary",), vmem_limit_bytes=VMEM_LIMIT),
        name="mixers_mlp",
    )(lgf, lgb, q, k, k, k, v, v, v, bias, rq, rk, qf, qb, kf, rv, gate, sb, rgains,
      x2, wo, gains, wu, wd, gfin)


def _rotary_tables(seq):
    half = RET_QK_DIM // 2
    inv = 1.0 / (ROPE_BASE ** (jnp.arange(0, RET_QK_DIM, 2, dtype=F32) / RET_QK_DIM))
    ang = jnp.arange(seq).astype(F32)[:, None] * inv[None, :]
    cos = jnp.cos(ang)
    sin = jnp.sin(ang)
    cos_t = jnp.tile(cos, (1, LANES // half))
    sin_t = jnp.tile(jnp.concatenate([-sin, sin], axis=-1), (1, LANES // RET_QK_DIM))
    return cos_t, sin_t


def kernel(x, w_in, w_out, na_rpb, ret_decay_fwd, ret_decay_bwd, ret_norm_gain,
           norm_mix, norm_mlp, w_up, w_down, norm_final):
    bsz, seq, d = x.shape
    depth = w_in.shape[0]
    n = bsz * seq
    rows = seq // GRID_W
    assert TM_PROJ % RET_CHUNK == 0 and seq % TM_PROJ == 0
    assert seq % (RET_BWD_CHUNKS_PER_STEP * RET_CHUNK) == 0
    assert (NA_ROWS_PER_STEP * GRID_W) % RET_CHUNK == 0
    assert rows % NA_ROWS_PER_STEP == 0 and rows >= NA_ROWS and NA_ROWS_PER_STEP >= NA_ROWS // 2
    assert w_up.shape[-1] % FF_CHUNK == 0 and NA_ROWS_PER_STEP % (w_up.shape[-1] // FF_CHUNK) == 0

    cos_t, sin_t = _rotary_tables(seq)
    w_in_b, w_out_b, w_up_b, w_down_b = (w.astype(BF16) for w in (w_in, w_out, w_up, w_down))
    lgf = jnp.log1p(-jnp.exp(ret_decay_fwd.astype(F32)))
    lgb = jnp.log1p(-jnp.exp(ret_decay_bwd.astype(F32)))
    bias = _natten_bias_table(na_rpb)
    g_mix = norm_mix.astype(F32).reshape(depth, 1, d)
    g_mlp = norm_mlp.astype(F32).reshape(depth, 1, d)
    g_ret = ret_norm_gain.astype(F32).reshape(depth, 1, RET_V_WIDTH)
    g_fin = norm_final.astype(F32).reshape(1, d)

    x2 = x.reshape(n, d)
    as_seq = lambda t: t.reshape(bsz, seq, t.shape[-1])
    for l in range(depth):
        q, k, v, rq, rk, qf, qb, kf, kb, rv, gate = _in_proj(
            l, lgf, lgb, x2, g_mix, w_in_b, cos_t, sin_t, seq)
        sb = _ret_bwd(l, lgb, as_seq(kb), as_seq(rv))
        x2 = _mixers_mlp(l, lgf, lgb, q, as_seq(k), as_seq(v), bias, rq, rk, qf, qb, kf, rv, gate, sb, g_ret,
                         x2, w_out_b, g_mlp, w_up_b, w_down_b, g_fin, l == depth - 1)
    return x2.reshape(bsz, seq, d)
```

```python
import functools

import jax
import jax.numpy as jnp
import numpy as np
from jax import lax
from jax.experimental import pallas as pl
from jax.experimental.pallas import tpu as pltpu

F32 = jnp.float32
BF16 = jnp.bfloat16

GRID_W = 64
NA_HEADS = 8
NA_HEAD_DIM = 64
NA_WIDTH = NA_HEADS * NA_HEAD_DIM
NA_ROWS = 8
NA_COLS = 16
RET_HEADS = 4
RET_QK_DIM = 64
RET_V_DIM = 128
RET_QK_WIDTH = RET_HEADS * RET_QK_DIM
RET_V_WIDTH = RET_HEADS * RET_V_DIM
RET_CHUNK = 128
ROPE_BASE = 10000.0
EPS = 1e-6
NEG = -1e30

LANES = 128
HEAD_PAIRS_NA = NA_WIDTH // LANES
HEAD_PAIRS_RET = RET_QK_WIDTH // LANES
VMEM_LIMIT = 60 * 1024 * 1024

TM_PROJ = 1024
FF_CHUNK = 1024
NA_ROWS_PER_STEP = 8
RET_BWD_CHUNKS_PER_STEP = 16

_NT = (((1,), (1,)), ((), ()))
_TN = (((0,), (0,)), ((), ()))


def _rms(x, gain):
    return x * lax.rsqrt(jnp.mean(x * x, axis=-1, keepdims=True) + EPS) * gain


def _lane_is_second_head():
    return lax.broadcasted_iota(jnp.int32, (1, LANES), 1) >= NA_HEAD_DIM


def _pair_lane_values(smem_ref, layer, p):
    return jnp.where(_lane_is_second_head(), smem_ref[layer, 2 * p + 1], smem_ref[layer, 2 * p])


def _second_head_rows():
    return lax.broadcasted_iota(jnp.int32, (LANES, 1), 0) >= RET_QK_DIM


def _pair_row_values(smem_ref, layer, p):
    return jnp.where(_second_head_rows(), smem_ref[layer, 2 * p + 1], smem_ref[layer, 2 * p])


def _layer_block(a, layer, grid_rank, **kw):
    zeros = (0,) * (a.ndim - 1)
    if grid_rank == 1:
        index_map = lambda i: (layer,) + zeros
    else:
        index_map = lambda i, j: (layer,) + zeros
    return pl.BlockSpec((None,) + a.shape[1:], index_map, **kw)


def _in_proj_kernel(lgf_ref, lgb_ref, x_ref, gain_ref, w_ref, cos_ref, sin_ref,
                    q_ref, k_ref, v_ref, rq_ref, rk_ref, qf_ref, qb_ref, kf_ref, kb_ref, rv_ref, rg_ref,
                    decay_ref, *, layer):
    c = RET_CHUNK

    @pl.when(pl.program_id(0) == 0)
    def _():
        pos = lax.broadcasted_iota(jnp.int32, (c, 1), 0).astype(F32)
        for p in range(HEAD_PAIRS_RET):
            lanes = slice(p * LANES, (p + 1) * LANES)
            lf = _pair_lane_values(lgf_ref, layer, p)
            lb = _pair_lane_values(lgb_ref, layer, p)
            decay_ref[0, :, lanes] = jnp.exp(lf * (pos + 1.0))
            decay_ref[1, :, lanes] = jnp.exp(lb * (float(c) - pos))
            decay_ref[2, :, lanes] = jnp.exp(lf * (float(c) - 1.0 - pos))
            decay_ref[3, :, lanes] = jnp.exp(lb * pos)

    h = _rms(x_ref[...], gain_ref[...]).astype(BF16)
    tm = h.shape[0]

    def decayed(t, which):
        t3 = t.reshape(tm // c, c, RET_QK_WIDTH) * decay_ref[which][None]
        return t3.reshape(tm, RET_QK_WIDTH).astype(BF16)

    def proj(c0, c1):
        return jnp.dot(h, w_ref[:, c0:c1], preferred_element_type=F32)

    starts = np.cumsum([0, NA_WIDTH, NA_WIDTH, NA_WIDTH, RET_QK_WIDTH, RET_QK_WIDTH, RET_V_WIDTH])
    c_q, c_k, c_v, c_rq, c_rk, c_rv, c_rg = (int(s) for s in starts)

    cos = cos_ref[...]
    sin = sin_ref[...]
    lane = lax.broadcasted_iota(jnp.int32, (1, LANES), 1)
    first_half = (lane % RET_QK_DIM) < RET_QK_DIM // 2

    def rotary(t, scale):
        parts = []
        for p in range(HEAD_PAIRS_RET):
            tp = t[:, p * LANES:(p + 1) * LANES]
            swapped = jnp.where(first_half,
                                pltpu.roll(tp, LANES - RET_QK_DIM // 2, 1),
                                pltpu.roll(tp, RET_QK_DIM // 2, 1))
            parts.append((tp * cos + swapped * sin) * scale)
        return jnp.concatenate(parts, axis=-1)

    rqk = proj(c_rq, c_rk + RET_QK_WIDTH)
    rq = rotary(rqk[:, :RET_QK_WIDTH], 1.0)
    rq_ref[...] = rq.astype(BF16)
    qf_ref[...] = decayed(rq, 0)
    qb_ref[...] = decayed(rq, 1)
    rk = rotary(rqk[:, RET_QK_WIDTH:], RET_QK_DIM ** -0.5)
    rk_ref[...] = rk.astype(BF16)
    kf_ref[...] = decayed(rk, 2)
    kb_ref[...] = decayed(rk, 3)
    g = proj(c_rg, c_rg + RET_V_WIDTH)
    rg_ref[...] = (g / (1.0 + jnp.exp(-g))).astype(BF16)
    rv_ref[...] = proj(c_rv, c_rv + RET_V_WIDTH).astype(BF16)
    q_ref[...] = (proj(c_q, c_q + NA_WIDTH) * NA_HEAD_DIM ** -0.5).astype(BF16)
    k_ref[...] = proj(c_k, c_k + NA_WIDTH).astype(BF16)
    v_ref[...] = proj(c_v, c_v + NA_WIDTH).astype(BF16)


def _in_proj(layer, lgf, lgb, x2, gains, w, cos_t, sin_t, seq):
    n, d = x2.shape
    tm = TM_PROJ
    pos_blocks = seq // tm
    tok = lambda i: (i, 0)
    smem = pl.BlockSpec(memory_space=pltpu.SMEM)
    widths = (NA_WIDTH,) * 3 + (RET_QK_WIDTH,) * 6 + (RET_V_WIDTH,) * 2
    return pl.pallas_call(
        functools.partial(_in_proj_kernel, layer=layer),
        out_shape=[jax.ShapeDtypeStruct((n, w_), BF16) for w_ in widths],
        grid=(n // tm,),
        in_specs=[
            smem, smem,
            pl.BlockSpec((tm, d), tok),
            _layer_block(gains, layer, 1),
            _layer_block(w, layer, 1, pipeline_mode=pl.Buffered(1)),
            pl.BlockSpec((tm, LANES), lambda i: (i % pos_blocks, 0)),
            pl.BlockSpec((tm, LANES), lambda i: (i % pos_blocks, 0)),
        ],
        out_specs=[pl.BlockSpec((tm, w_), tok) for w_ in widths],
        scratch_shapes=[pltpu.VMEM((4, RET_CHUNK, RET_QK_WIDTH), F32)],
        compiler_params=pltpu.CompilerParams(
            dimension_semantics=("arbitrary",), vmem_limit_bytes=VMEM_LIMIT),
        name="in_proj",
    )(lgf, lgb, x2, gains, w, cos_t, sin_t)


def _natten_bias_table(rpb):
    depth = rpb.shape[0]
    qc = np.arange(GRID_W)[:, None]
    kc = np.arange(GRID_W)[None, :]
    dc = np.clip(kc - qc, -(NA_COLS - 1), NA_COLS - 1) + NA_COLS - 1
    lo = np.clip(qc - NA_COLS // 2, 0, GRID_W - NA_COLS)
    valid = (kc >= lo) & (kc < lo + NA_COLS)
    onehot = (dc[None] == np.arange(2 * NA_COLS - 1)[:, None, None]).astype(np.float32)
    t = jnp.einsum('lhdc,cqk->lhdqk', rpb.astype(F32), jnp.asarray(onehot),
                   precision=lax.Precision.HIGHEST)
    t = jnp.where(jnp.asarray(valid), t, NEG)

    def row_pairs(u):
        return jnp.concatenate([u[:, :, :-1], u[:, :, 1:]], axis=-1)

    n_off = 2 * NA_ROWS - 2
    t2 = row_pairs(t).reshape(depth, HEAD_PAIRS_NA, 2, n_off, GRID_W, 2 * GRID_W)
    return jnp.transpose(t2, (0, 1, 3, 2, 4, 5)).reshape(
        depth, HEAD_PAIRS_NA, n_off, 2 * GRID_W, 2 * GRID_W)


def _chunk_kv(k_decayed, v2):
    t = lax.dot_general(k_decayed, v2, _TN, preferred_element_type=F32)
    return jnp.where(_second_head_rows(), t[:, RET_V_DIM:], t[:, :RET_V_DIM])


def _per_head_states(state):
    second = _second_head_rows()
    return (jnp.where(second, 0.0, state).astype(BF16), jnp.where(second, state, 0.0).astype(BF16))


def _ret_bwd_kernel(lgb_ref, kb_ref, v_ref, sb_ref, state_ref, *, layer):
    @pl.when(pl.program_id(1) == 0)
    def _():
        state_ref[...] = jnp.zeros_like(state_ref)

    c = RET_CHUNK
    chunks = range(RET_BWD_CHUNKS_PER_STEP)
    pairs = range(HEAD_PAIRS_RET)
    upd = {}
    for p in pairs:
        for cc in chunks:
            rows = slice(cc * c, (cc + 1) * c)
            upd[cc, p] = _chunk_kv(kb_ref[0, rows, p * LANES:(p + 1) * LANES],
                                   v_ref[0, rows, 2 * p * RET_V_DIM:(2 * p + 2) * RET_V_DIM])
    for p in pairs:
        decay = jnp.exp(_pair_row_values(lgb_ref, layer, p) * float(c))
        state = state_ref[p]
        for cc in reversed(chunks):
            sb_ref[0, cc, p] = state.astype(BF16)
            state = decay * state + upd[cc, p]
        state_ref[p] = state


def _ret_bwd(layer, lgb, kb, rv):
    b, s, _ = kb.shape
    nch = RET_BWD_CHUNKS_PER_STEP
    tc = nch * RET_CHUNK
    steps = s // tc
    rev = lambda i, j: (i, steps - 1 - j, 0)
    return pl.pallas_call(
        functools.partial(_ret_bwd_kernel, layer=layer),
        out_shape=jax.ShapeDtypeStruct((b, s // RET_CHUNK, HEAD_PAIRS_RET, LANES, RET_V_DIM), BF16),
        grid=(b, steps),
        in_specs=[
            pl.BlockSpec(memory_space=pltpu.SMEM),
            pl.BlockSpec((1, tc, RET_QK_WIDTH), rev),
            pl.BlockSpec((1, tc, RET_V_WIDTH), rev),
        ],
        out_specs=pl.BlockSpec((1, nch, HEAD_PAIRS_RET, LANES, RET_V_DIM),
                               lambda i, j: (i, steps - 1 - j, 0, 0, 0)),
        scratch_shapes=[pltpu.VMEM((HEAD_PAIRS_RET, LANES, RET_V_DIM), F32)],
        compiler_params=pltpu.CompilerParams(
            dimension_semantics=("parallel", "arbitrary"), vmem_limit_bytes=VMEM_LIMIT),
        name="ret_bwd",
    )(lgb, kb, rv)


def _ret_decay_matrix(lgf_ref, lgb_ref, dmat_ref, layer):
    c = RET_CHUNK
    i = lax.broadcasted_iota(jnp.int32, (c, c), 0)
    j = lax.broadcasted_iota(jnp.int32, (c, c), 1)
    diff = (i - j).astype(F32)
    for h in range(RET_HEADS):
        dmat_ref[h // 2, :, (h % 2) * c:(h % 2 + 1) * c] = jnp.exp(
            jnp.where(diff >= 0, lgf_ref[layer, h] * diff, -lgb_ref[layer, h] * diff))


def _ret_scores_and_states(n_chunks, q_ref, k_ref, kf_ref, v_ref, lgf_ref, state_ref, dmat_ref, layer):
    c = RET_CHUNK
    second = _lane_is_second_head()
    chunks = range(n_chunks)
    pairs = range(HEAD_PAIRS_RET)
    rows = [slice(cc * c, (cc + 1) * c) for cc in chunks]
    cols = [slice(p * LANES, (p + 1) * LANES) for p in pairs]
    upd = {(cc, p): _chunk_kv(kf_ref[rows[cc], cols[p]],
                              v_ref[rows[cc], 2 * p * RET_V_DIM:(2 * p + 2) * RET_V_DIM])
           for cc in chunks for p in pairs}
    scores = {}
    for cc in chunks:
        for p in pairs:
            kp = k_ref[rows[cc], cols[p]]
            k2 = jnp.concatenate([jnp.where(second, jnp.zeros_like(kp), kp),
                                  jnp.where(second, kp, jnp.zeros_like(kp))], axis=0)
            s = lax.dot_general(q_ref[rows[cc], cols[p]], k2, _NT, preferred_element_type=F32)
            scores[cc, p] = (s * dmat_ref[p]).astype(BF16)
    before = {}
    for p in pairs:
        decay = jnp.exp(_pair_row_values(lgf_ref, layer, p) * float(c))
        state = state_ref[p]
        for cc in chunks:
            before[cc, 2 * p], before[cc, 2 * p + 1] = _per_head_states(state)
            state = decay * state + upd[cc, p]
        state_ref[p] = state
    return scores, before


def _ret_outputs(n_chunks, scores, before, qf_ref, qb_ref, v_ref, gate_ref, sb_ref, gain_ref, o_ref):
    c = RET_CHUNK
    second_rows = _second_head_rows()
    for cc in range(n_chunks):
        rows = slice(cc * c, (cc + 1) * c)
        for p in range(HEAD_PAIRS_RET):
            cols = slice(p * LANES, (p + 1) * LANES)
            qf = qf_ref[rows, cols]
            qb = qb_ref[rows, cols]
            sb = sb_ref[cc, p]
            zeros = jnp.zeros_like(sb)
            v0 = v_ref[rows, 2 * p * RET_V_DIM:(2 * p + 1) * RET_V_DIM]
            v1 = v_ref[rows, (2 * p + 1) * RET_V_DIM:(2 * p + 2) * RET_V_DIM]
            lhs = jnp.concatenate([scores[cc, p], qf, qb], axis=-1)
            rhs = jnp.concatenate([
                jnp.concatenate([v0, zeros], axis=-1),
                jnp.concatenate([zeros, v1], axis=-1),
                jnp.concatenate([before[cc, 2 * p], before[cc, 2 * p + 1]], axis=-1),
                jnp.concatenate([jnp.where(second_rows, zeros, sb), jnp.where(second_rows, sb, zeros)], axis=-1),
            ], axis=0)
            y2 = jnp.dot(lhs, rhs, preferred_element_type=F32)
            for hh in range(2):
                h = 2 * p + hh
                vcols = slice(h * RET_V_DIM, (h + 1) * RET_V_DIM)
                y = y2[:, hh * RET_V_DIM:(hh + 1) * RET_V_DIM]
                mu = jnp.mean(y, axis=-1, keepdims=True)
                yc = y - mu
                var = jnp.mean(yc * yc, axis=-1, keepdims=True)
                yn = yc * lax.rsqrt(var + EPS) * gain_ref[:, vcols]
                o_ref[rows, vcols] = (gate_ref[rows, vcols].astype(F32) * yn).astype(BF16)


def _mixers_mlp_kernel(lgf_ref, lgb_ref,
                       q_ref, k_prev_ref, k_cur_ref, k_next_ref, v_prev_ref, v_cur_ref, v_next_ref, bias_ref,
                       rq_ref, rk_ref, qf_ref, qb_ref, kf_ref, rv_ref, gate_ref, sb_ref, rgain_ref,
                       x_ref, wo_ref, gain_ref, wu_ref, wd_ref, gfin_ref,
                       o_ref, na_ref, ret_ref, kwin_ref, vwin_ref, state_ref, dmat_ref,
                       *, layer, rows, tiles_per_batch, n_tiles, final_norm):
    step = pl.program_id(0)
    tm = NA_ROWS_PER_STEP * GRID_W
    ret_chunks = tm // RET_CHUNK

    @pl.when(step == 0)
    def _():
        _ret_decay_matrix(lgf_ref, lgb_ref, dmat_ref, layer)

    @pl.when(step % tiles_per_batch == 0)
    def _():
        state_ref[...] = jnp.zeros_like(state_ref)

    def load_windows():
        for win_ref, parts in ((kwin_ref, (k_prev_ref, k_cur_ref, k_next_ref)),
                               (vwin_ref, (v_prev_ref, v_cur_ref, v_next_ref))):
            for i, part in enumerate(parts):
                win_ref[i * tm:(i + 1) * tm, :] = part[0]

    second = _lane_is_second_head()
    pairs = range(HEAD_PAIRS_NA)
    cols = [slice(p * LANES, (p + 1) * LANES) for p in pairs]
    keys = NA_ROWS * GRID_W
    tile = jnp.minimum(step, n_tiles - 1)
    r0 = (tile % tiles_per_batch) * NA_ROWS_PER_STEP

    def offsets(rr):
        r = r0 + rr
        rs = jnp.clip(r - NA_ROWS // 2, 0, rows - NA_ROWS)
        d0 = rs - r + NA_ROWS - 1
        return rr * GRID_W, pl.multiple_of((rs - r0 + NA_ROWS_PER_STEP) * GRID_W, GRID_W), d0

    def row_scores(rr):
        qoff, koff, d0 = offsets(rr)
        scores = []
        for p in pairs:
            qp = q_ref[qoff:qoff + GRID_W, cols[p]]
            q2 = jnp.concatenate([jnp.where(second, jnp.zeros_like(qp), qp),
                                  jnp.where(second, qp, jnp.zeros_like(qp))], axis=0)
            kp = kwin_ref[pl.ds(koff, keys), cols[p]]
            bias = jnp.concatenate([bias_ref[p, d0 + 2 * j] for j in range(NA_ROWS // 2)], axis=-1)
            scores.append(lax.dot_general(q2, kp, _NT, preferred_element_type=F32) + bias)
        return scores

    def softmax_terms(scores):
        terms = []
        for s in scores:
            e = jnp.exp(s - jnp.max(s, axis=-1, keepdims=True))
            terms.append((e.astype(BF16), jnp.sum(e, axis=-1, keepdims=True)))
        return terms

    def row_outputs(rr, terms):
        qoff, koff, _ = offsets(rr)
        for p in pairs:
            e, denom = terms[p]
            vp = vwin_ref[pl.ds(koff, keys), cols[p]]
            o2 = jnp.dot(e, vp, preferred_element_type=F32) / denom
            out = jnp.where(second, o2[GRID_W:], o2[:GRID_W])
            na_ref[qoff:qoff + GRID_W, cols[p]] = out.astype(BF16)

    d_ff = wu_ref.shape[1]
    n_chunks = d_ff // FF_CHUNK
    rows_per_chunk = NA_ROWS_PER_STEP // n_chunks
    chunk_rows = lambda c: range(c * rows_per_chunk, (c + 1) * rows_per_chunk)

    def body(attend, dense):
        if dense:
            mix = (jnp.dot(na_ref[...], wo_ref[:NA_WIDTH, :], preferred_element_type=F32)
                   + jnp.dot(ret_ref[...], wo_ref[NA_WIDTH:, :], preferred_element_type=F32))
            x1 = x_ref[...] + mix
            h = _rms(x1, gain_ref[...]).astype(BF16)
        if attend:
            ret_scores, ret_before = _ret_scores_and_states(
                ret_chunks, rq_ref, rk_ref, kf_ref, rv_ref, lgf_ref, state_ref, dmat_ref, layer)
            load_windows()
            scores = {rr: row_scores(rr) for rr in chunk_rows(0)}
        mlp = None
        for c in range(n_chunks):
            f0 = c * FF_CHUNK
            if dense:
                u = jnp.maximum(jnp.dot(h, wu_ref[:, f0:f0 + FF_CHUNK], preferred_element_type=F32), 0.0)
            if attend:
                for rr in chunk_rows(c):
                    row_outputs(rr, softmax_terms(scores.pop(rr)))
                if c + 1 < n_chunks:
                    for rr in chunk_rows(c + 1):
                        scores[rr] = row_scores(rr)
            if dense:
                t = jnp.dot((u * u).astype(BF16), wd_ref[f0:f0 + FF_CHUNK, :], preferred_element_type=F32)
                mlp = t if mlp is None else mlp + t
            if attend and c == 0:
                _ret_outputs(ret_chunks, ret_scores, ret_before, qf_ref, qb_ref, rv_ref, gate_ref, sb_ref,
                             rgain_ref, ret_ref)
        if dense:
            x3 = x1 + mlp
            o_ref[...] = _rms(x3, gfin_ref[...]) if final_norm else x3

    pl.when(step == 0)(functools.partial(body, True, False))
    pl.when(jnp.logical_and(step > 0, step < n_tiles))(functools.partial(body, True, True))
    pl.when(step == n_tiles)(functools.partial(body, False, True))


def _mixers_mlp(layer, lgf, lgb, q, k, v, bias, rq, rk, qf, qb, kf, rv, gate, sb, rgains,
                x2, wo, gains, wu, wd, gfin, final_norm):
    n, d = x2.shape
    b, s, w = k.shape
    rows = s // GRID_W
    tm = NA_ROWS_PER_STEP * GRID_W
    n_tiles = n // tm
    tiles_per_batch = s // tm
    ret_chunks = tm // RET_CHUNK
    att = lambda i: jnp.minimum(i, n_tiles - 1)
    prev = lambda i: jnp.maximum(i - 1, 0)
    resident = lambda a: _layer_block(a, layer, 1, pipeline_mode=pl.Buffered(1))
    smem = pl.BlockSpec(memory_space=pltpu.SMEM)
    att_tile = lambda width: pl.BlockSpec((tm, width), lambda i: (att(i), 0))
    sb = sb.reshape((b * (s // RET_CHUNK),) + sb.shape[2:])

    def neighbour_tile(shift):
        def index_map(i):
            t = att(i)
            tb = jnp.clip(t % tiles_per_batch + shift, 0, tiles_per_batch - 1)
            return (t // tiles_per_batch, tb, 0)
        return pl.BlockSpec((1, tm, w), index_map)

    halo = [neighbour_tile(-1), neighbour_tile(0), neighbour_tile(1)]
    return pl.pallas_call(
        functools.partial(_mixers_mlp_kernel, layer=layer, rows=rows, tiles_per_batch=tiles_per_batch,
                          n_tiles=n_tiles, final_norm=final_norm),
        out_shape=jax.ShapeDtypeStruct((n, d), F32),
        grid=(n_tiles + 1,),
        in_specs=[
            smem, smem,
            att_tile(w),
            *halo,
            *halo,
            resident(bias),
            att_tile(RET_QK_WIDTH), att_tile(RET_QK_WIDTH), att_tile(RET_QK_WIDTH), att_tile(RET_QK_WIDTH),
            att_tile(RET_QK_WIDTH), att_tile(RET_V_WIDTH), att_tile(RET_V_WIDTH),
            pl.BlockSpec((ret_chunks, HEAD_PAIRS_RET, LANES, RET_V_DIM), lambda i: (att(i), 0, 0, 0)),
            _layer_block(rgains, layer, 1),
            pl.BlockSpec((tm, d), lambda i: (prev(i), 0)),
            resident(wo),
            _layer_block(gains, layer, 1),
            resident(wu),
            resident(wd),
            pl.BlockSpec((1, d), lambda i: (0, 0)),
        ],
        out_specs=pl.BlockSpec((tm, d), lambda i: (prev(i), 0)),
        scratch_shapes=[pltpu.VMEM((tm, NA_WIDTH), BF16),
                        pltpu.VMEM((tm, RET_V_WIDTH), BF16),
                        pltpu.VMEM((3 * tm, NA_WIDTH), BF16),
                        pltpu.VMEM((3 * tm, NA_WIDTH), BF16),
                        pltpu.VMEM((HEAD_PAIRS_RET, LANES, RET_V_DIM), F32),
                        pltpu.VMEM((HEAD_PAIRS_RET, RET_CHUNK, 2 * RET_CHUNK), F32)],
        compiler_params=pltpu.CompilerParams(
            dimension_semantics=("arbitrary",), vmem_limit_bytes=VMEM_LIMIT),
        name="mixers_mlp",
    )(lgf, lgb, q, k, k, k, v, v, v, bias, rq, rk, qf, qb, kf, rv, gate, sb, rgains,
      x2, wo, gains, wu, wd, gfin)


def _rotary_tables(seq):
    half = RET_QK_DIM // 2
    inv = 1.0 / (ROPE_BASE ** (jnp.arange(0, RET_QK_DIM, 2, dtype=F32) / RET_QK_DIM))
    ang = jnp.arange(seq).astype(F32)[:, None] * inv[None, :]
    cos = jnp.cos(ang)
    sin = jnp.sin(ang)
    cos_t = jnp.tile(cos, (1, LANES // half))
    sin_t = jnp.tile(jnp.concatenate([-sin, sin], axis=-1), (1, LANES // RET_QK_DIM))
    return cos_t, sin_t


def kernel(x, w_in, w_out, na_rpb, ret_decay_fwd, ret_decay_bwd, ret_norm_gain,
           norm_mix, norm_mlp, w_up, w_down, norm_final):
    bsz, seq, d = x.shape
    depth = w_in.shape[0]
    n = bsz * seq
    rows = seq // GRID_W
    assert TM_PROJ % RET_CHUNK == 0 and seq % TM_PROJ == 0
    assert seq % (RET_BWD_CHUNKS_PER_STEP * RET_CHUNK) == 0
    assert (NA_ROWS_PER_STEP * GRID_W) % RET_CHUNK == 0
    assert rows % NA_ROWS_PER_STEP == 0 and rows >= NA_ROWS and NA_ROWS_PER_STEP >= NA_ROWS // 2
    assert w_up.shape[-1] % FF_CHUNK == 0 and NA_ROWS_PER_STEP % (w_up.shape[-1] // FF_CHUNK) == 0

    cos_t, sin_t = _rotary_tables(seq)
    w_in_b, w_out_b, w_up_b, w_down_b = (w.astype(BF16) for w in (w_in, w_out, w_up, w_down))
    lgf = jnp.log1p(-jnp.exp(ret_decay_fwd.astype(F32)))
    lgb = jnp.log1p(-jnp.exp(ret_decay_bwd.astype(F32)))
    bias = _natten_bias_table(na_rpb)
    g_mix = norm_mix.astype(F32).reshape(depth, 1, d)
    g_mlp = norm_mlp.astype(F32).reshape(depth, 1, d)
    g_ret = ret_norm_gain.astype(F32).reshape(depth, 1, RET_V_WIDTH)
    g_fin = norm_final.astype(F32).reshape(1, d)

    x2 = x.reshape(n, d)
    as_seq = lambda t: t.reshape(bsz, seq, t.shape[-1])
    for l in range(depth):
        q, k, v, rq, rk, qf, qb, kf, kb, rv, gate = _in_proj(
            l, lgf, lgb, x2, g_mix, w_in_b, cos_t, sin_t, seq)
        sb = _ret_bwd(l, lgb, as_seq(kb), as_seq(rv))
        x2 = _mixers_mlp(l, lgf, lgb, q, as_seq(k), as_seq(v), bias, rq, rk, qf, qb, kf, rv, gate, sb, g_ret,
                         x2, w_out_b, g_mlp, w_up_b, w_down_b, g_fin, l == depth - 1)
    return x2.reshape(bsz, seq, d)
```
